```python
import functools
import jax, jax.numpy as jnp
from jax import lax
import numpy as np

D_MODEL = 1024
BATCH = 16
SEQ = 4096
DEPTH = 2
DEC_BATCH = 16
DEC_SEQ = 64
PAST_LEN = 1024

CHUNK = 64
M_HEADS = 4
M_HDIM = D_MODEL // 4
M_WIDTH = M_HEADS * M_HDIM
A_HEADS = 8
A_HDIM = 64
A_WIDTH = A_HEADS * A_HDIM
PAST_CHUNKS = 8
ATT_REACH = PAST_CHUNKS * CHUNK
BAND = ATT_REACH + CHUNK
REL_CLIP = 128
R_HEADS = 8
N_KEYS = 128
N_EXPERTS = N_KEYS * N_KEYS
KEY_DIM = 256
HALF_KEY = KEY_DIM // 2
TOPK = 16
PEER_BLOCK = 256
PLE_DIM = 256
ALPHA = (2 * DEPTH) ** 0.25
BETA = (8 * DEPTH) ** -0.25
LN_EPS = 1e-5

SPLITS = (M_WIDTH, M_WIDTH, M_WIDTH, M_HEADS, M_HEADS, M_WIDTH,
          A_WIDTH, A_WIDTH, A_WIDTH,
          D_MODEL, D_MODEL)
IN_COLS = sum(SPLITS)
SPLIT_IDX = tuple(int(s) for s in np.cumsum(SPLITS)[:-1])

kernel_name = 'hybrid_mlstm_bandattn_peer_stream_step'


def layer_norm(x, g, b):
    xf = x.astype(jnp.float32)
    mu = xf.mean(-1, keepdims=True)
    var = jnp.square(xf - mu).mean(-1, keepdims=True)
    return ((xf - mu) * lax.rsqrt(var + LN_EPS) * g.astype(jnp.float32) + b.astype(jnp.float32)).astype(x.dtype)


def head_norm(h, g):
    mu = h.mean(-1, keepdims=True)
    var = jnp.square(h - mu).mean(-1, keepdims=True)
    return (h - mu) * lax.rsqrt(var + LN_EPS) * g.astype(jnp.float32)


def mlstm_chunk(state, inp):
    C, n, m = (t.astype(jnp.float32) for t in state)
    q, k, v, ig, lf = (t.astype(jnp.float32) for t in inp)
    L = q.shape[1]
    b = jnp.cumsum(lf, axis=1)
    inter = b + m[:, None, :]
    causal = jnp.tril(jnp.ones((L, L), dtype=bool))
    intra = b[:, :, None, :] - b[:, None, :, :] + ig[:, None, :, :]
    intra = jnp.where(causal[None, :, :, None], intra, -jnp.inf)
    m_t = jnp.maximum(inter, intra.max(axis=2))
    w_inter = jnp.exp(inter - m_t)
    w_intra = jnp.exp(intra - m_t[:, :, None, :])
    a = w_intra * jnp.einsum('bthd,bshd->btsh', q, k)
    num = w_inter[..., None] * jnp.einsum('bhvk,bthk->bthv', C, q) + jnp.einsum('btsh,bshv->bthv', a, v)
    den = w_inter * jnp.einsum('bhk,bthk->bth', n, q) + a.sum(axis=2)
    h = num / jnp.maximum(jnp.abs(den), jnp.exp(-m_t))[..., None]
    m_new = m_t[:, -1]
    g_state = jnp.exp(b[:, -1] + m - m_new)
    g_s = jnp.exp(b[:, -1:, :] - b + ig - m_new[:, None, :])
    C_new = g_state[..., None, None] * C + jnp.einsum('bsh,bshv,bshk->bhvk', g_s, v, k)
    n_new = g_state[..., None] * n + jnp.einsum('bsh,bshk->bhk', g_s, k)
    return (C_new, n_new, m_new), h


def mlstm_prompt(q, k, v, ig, lf):
    B, S = q.shape[:2]
    nc = S // CHUNK
    to_chunks = lambda t: jnp.swapaxes(t.reshape((B, nc, CHUNK) + t.shape[2:]), 0, 1)
    init = (jnp.zeros((B, M_HEADS, M_HDIM, M_HDIM), jnp.float32),
            jnp.zeros((B, M_HEADS, M_HDIM), jnp.float32),
            jnp.zeros((B, M_HEADS), jnp.float32))
    state, h = lax.scan(mlstm_chunk, init, tuple(to_chunks(t) for t in (q, k, v, ig, lf)))
    return jnp.swapaxes(h, 0, 1).reshape(B, S, M_HEADS, M_HDIM), state


def mlstm_sample(q, k, v, ig, lf, C, n, m):
    state, h = mlstm_chunk((C, n, m), (q, k, v, ig, lf))
    return h, state


def band_attention(q, k, v, q_pos, k_pos, rel_bias):
    qc = q_pos // CHUNK
    kc = k_pos // CHUNK
    visible = (k_pos[None, :] >= 0) & (kc[None, :] <= qc[:, None]) & (kc[None, :] >= qc[:, None] - PAST_CHUNKS)
    rel = jnp.clip(q_pos[:, None] - k_pos[None, :], -REL_CLIP, REL_CLIP) + REL_CLIP
    bias = rel_bias.astype(jnp.float32)[:, rel]
    s = jnp.einsum('bqhd,bkhd->bhqk', q, k).astype(jnp.float32) * (A_HDIM ** -0.5) + bias[None]
    s = jnp.where(visible[None, None], s, -jnp.inf)
    p = jax.nn.softmax(s, axis=-1)
    return jnp.einsum('bhqk,bkhd->bqhd', p.astype(v.dtype), v)


def attn_prompt(q, k, v, rel_bias):
    B, S = q.shape[:2]
    nc = S // CHUNK
    pad = ((0, 0), (ATT_REACH, 0), (0, 0), (0, 0))
    kp, vp = jnp.pad(k, pad), jnp.pad(v, pad)

    def one_chunk(c):
        start = c * CHUNK
        qb = lax.dynamic_slice_in_dim(q, start, CHUNK, axis=1)
        kb = lax.dynamic_slice_in_dim(kp, start, BAND, axis=1)
        vb = lax.dynamic_slice_in_dim(vp, start, BAND, axis=1)
        q_pos = start + jnp.arange(CHUNK)
        k_pos = start - ATT_REACH + jnp.arange(BAND)
        return band_attention(qb, kb, vb, q_pos, k_pos, rel_bias)

    o = lax.map(one_chunk, jnp.arange(nc))
    o = jnp.swapaxes(o, 0, 1).reshape(B, S, A_HEADS, A_HDIM)
    keep = min(ATT_REACH, S)
    return o, k[:, S - keep:], v[:, S - keep:]


def attn_sample(q, k, v, rel_bias, cache_k, cache_v):
    n_cache = cache_k.shape[1]
    L = q.shape[1]
    kc = jnp.concatenate([cache_k.astype(k.dtype), k], axis=1)
    vc = jnp.concatenate([cache_v.astype(v.dtype), v], axis=1)
    q_pos = PAST_LEN + jnp.arange(L)
    k_pos = PAST_LEN - n_cache + jnp.arange(n_cache + L)
    return band_attention(q, kc, vc, q_pos, k_pos, rel_bias), k, v


def token_mixer(x, w_in, if_bias, m_norm, rel_bias, w_bm, w_ba, w_o, mlstm_core, attn_core):
    B, L, _ = x.shape
    z = x @ w_in
    mq, mk, mv, mi, mf, mo, aq, ak, av, gm, ga = jnp.split(z, SPLIT_IDX, axis=-1)
    mh = lambda t: t.reshape(B, L, M_HEADS, M_HDIM)
    ah = lambda t: t.reshape(B, L, A_HEADS, A_HDIM)
    ig = (mi + if_bias[0]).astype(jnp.float32)
    lf = jax.nn.log_sigmoid((mf + if_bias[1]).astype(jnp.float32))
    h, mstate = mlstm_core(mh(mq), mh(mk) * (M_HDIM ** -0.5), mh(mv), ig, lf)
    h = head_norm(h, m_norm).astype(x.dtype).reshape(B, L, M_WIDTH)
    y_m = (jax.nn.sigmoid(mo) * h) @ w_bm
    o_a, k_new, v_new = attn_core(ah(aq), ah(ak), ah(av), rel_bias)
    y_a = o_a.reshape(B, L, A_WIDTH) @ w_ba
    mix = (jax.nn.sigmoid(gm) * y_m + jax.nn.sigmoid(ga) * y_a) @ w_o
    return mix, mstate, (k_new, v_new)


def peer_ffn(x, wq, subkeys, u_tab, v_tab):
    B, L, D = x.shape
    T = B * L
    nblk = -(-T // PEER_BLOCK)
    xt = jnp.pad(x.reshape(T, D), ((0, nblk * PEER_BLOCK - T), (0, 0))).reshape(nblk, PEER_BLOCK, D)

    def block(xb):
        qr = (xb @ wq).reshape(PEER_BLOCK, R_HEADS, 2, HALF_KEY)
        s = jnp.einsum('thpd,hpnd->thpn', qr, subkeys).astype(jnp.float32)
        sv, si = lax.top_k(s, TOPK)
        cand = (sv[:, :, 0, :, None] + sv[:, :, 1, None, :]).reshape(PEER_BLOCK, R_HEADS, TOPK * TOPK)
        cidx = (si[:, :, 0, :, None] * N_KEYS + si[:, :, 1, None, :]).reshape(PEER_BLOCK, R_HEADS, TOPK * TOPK)
        fv, fi = lax.top_k(cand, TOPK)
        eidx = jnp.take_along_axis(cidx, fi, axis=-1)
        g = jax.nn.softmax(fv, axis=-1)
        act = jax.nn.gelu(jnp.einsum('thkd,td->thk', u_tab[eidx], xb).astype(jnp.float32))
        w = (g * act).astype(xb.dtype)
        return jnp.einsum('thk,thkd->td', w, v_tab[eidx])

    return lax.map(block, xt).reshape(nblk * PEER_BLOCK, D)[:T].reshape(B, L, D)


def encoder_layer(x, pe, lw, mlstm_core, attn_core):
    (w_in, if_b, m_norm, rel_b, w_bm, w_ba, w_o, ln1g, ln1b, ln2g, ln2b,
     pq, psk, pu, pv, plp, plg) = lw
    mix, mstate, kv = token_mixer(x, w_in, if_b, m_norm, rel_b, w_bm, w_ba, w_o, mlstm_core, attn_core)
    x = layer_norm(ALPHA * x + mix, ln1g, ln1b)
    x = layer_norm(ALPHA * x + peer_ffn(x, pq, psk, pu, pv), ln2g, ln2b)
    x = x + (pe @ plp) * jax.nn.sigmoid(x @ plg)
    return x, mstate, kv


def setup_inputs(seed: int = 0) -> dict:
    key = jax.random.key(seed)
    ks = iter(jax.random.split(key, 40))
    nrm = lambda shape, scale: jax.random.normal(next(ks), shape, jnp.float32) * scale
    att_cache = min(ATT_REACH, PAST_LEN)
    f_bias = jnp.broadcast_to(jnp.linspace(3.0, 6.0, M_HEADS), (DEPTH, M_HEADS)) + nrm((DEPTH, M_HEADS), 0.1)
    i_bias = nrm((DEPTH, M_HEADS), 0.1)
    return {
        'x_prompt': nrm((BATCH, SEQ, D_MODEL), 1.0),
        'x_sample': nrm((DEC_BATCH, DEC_SEQ, D_MODEL), 1.0),
        'cache_attn_k': nrm((DEPTH, DEC_BATCH, att_cache, A_HEADS, A_HDIM), 1.0),
        'cache_attn_v': nrm((DEPTH, DEC_BATCH, att_cache, A_HEADS, A_HDIM), 1.0),
        'state_mlstm_C': nrm((DEPTH, DEC_BATCH, M_HEADS, M_HDIM, M_HDIM), 0.5),
        'state_mlstm_n': nrm((DEPTH, DEC_BATCH, M_HEADS, M_HDIM), 0.5),
        'state_mlstm_m': nrm((DEPTH, DEC_BATCH, M_HEADS), 1.0),
        'p_prompt': nrm((DEPTH, BATCH, SEQ, PLE_DIM), 1.0),
        'p_sample': nrm((DEPTH, DEC_BATCH, DEC_SEQ, PLE_DIM), 1.0),
        'w_in': nrm((DEPTH, D_MODEL, IN_COLS), D_MODEL ** -0.5),
        'mlstm_if_bias': jnp.stack([i_bias, f_bias], axis=1),
        'mlstm_norm_w': 1.0 + nrm((DEPTH, M_HEADS, M_HDIM), 0.05),
        'attn_rel_bias': nrm((DEPTH, A_HEADS, 2 * REL_CLIP + 1), 0.5),
        'w_branch_m': nrm((DEPTH, M_WIDTH, D_MODEL), BETA * M_WIDTH ** -0.5),
        'w_branch_a': nrm((DEPTH, A_WIDTH, D_MODEL), BETA * A_WIDTH ** -0.5),
        'w_out': nrm((DEPTH, D_MODEL, D_MODEL), BETA * D_MODEL ** -0.5),
        'ln1_g': 1.0 + nrm((DEPTH, D_MODEL), 0.05),
        'ln1_b': nrm((DEPTH, D_MODEL), 0.02),
        'ln2_g': 1.0 + nrm((DEPTH, D_MODEL), 0.05),
        'ln2_b': nrm((DEPTH, D_MODEL), 0.02),
        'peer_wq': nrm((DEPTH, D_MODEL, R_HEADS * KEY_DIM), D_MODEL ** -0.5),
        'peer_subkeys': nrm((DEPTH, R_HEADS, 2, N_KEYS, HALF_KEY), HALF_KEY ** -0.5),
        'peer_u': nrm((DEPTH, N_EXPERTS, D_MODEL), D_MODEL ** -0.5),
        'peer_v': nrm((DEPTH, N_EXPERTS, D_MODEL), BETA * R_HEADS ** -0.5),
        'ple_proj': nrm((DEPTH, PLE_DIM, D_MODEL), BETA * PLE_DIM ** -0.5),
        'ple_gate': nrm((DEPTH, D_MODEL, D_MODEL), D_MODEL ** -0.5),
    }


def reference(x_prompt, x_sample, cache_attn_k, cache_attn_v, state_mlstm_C, state_mlstm_n,
              state_mlstm_m, p_prompt, p_sample, w_in, mlstm_if_bias, mlstm_norm_w, attn_rel_bias,
              w_branch_m, w_branch_a, w_out, ln1_g, ln1_b, ln2_g, ln2_b, peer_wq, peer_subkeys,
              peer_u, peer_v, ple_proj, ple_gate):
    layer_w = (w_in, mlstm_if_bias, mlstm_norm_w, attn_rel_bias, w_branch_m, w_branch_a, w_out,
               ln1_g, ln1_b, ln2_g, ln2_b, peer_wq, peer_subkeys, peer_u, peer_v, ple_proj, ple_gate)
    yp, ys = x_prompt, x_sample
    kp_l, vp_l, Cp_l, np_l, mp_l = [], [], [], [], []
    ks_l, vs_l, Cs_l, ns_l, ms_l = [], [], [], [], []
    for i in range(DEPTH):
        lw = tuple(w[i] for w in layer_w)
        yp, (Cp, n_p, m_p), (kp, vp) = encoder_layer(yp, p_prompt[i], lw, mlstm_prompt, attn_prompt)
        ys, (Cs, n_s, m_s), (ksm, vsm) = encoder_layer(
            ys, p_sample[i], lw,
            functools.partial(mlstm_sample, C=state_mlstm_C[i], n=state_mlstm_n[i], m=state_mlstm_m[i]),
            functools.partial(attn_sample, cache_k=cache_attn_k[i], cache_v=cache_attn_v[i]))
        kp_l.append(kp); vp_l.append(vp); Cp_l.append(Cp); np_l.append(n_p); mp_l.append(m_p)
        ks_l.append(ksm); vs_l.append(vsm); Cs_l.append(Cs); ns_l.append(n_s); ms_l.append(m_s)
    st = lambda l: jnp.stack(l, axis=0)
    return (yp, ys, st(kp_l), st(vp_l), st(Cp_l), st(np_l), st(mp_l),
            st(ks_l), st(vs_l), st(Cs_l), st(ns_l), st(ms_l))
```

```python
import functools
import math

import jax
import jax.numpy as jnp
from jax import lax
from jax.experimental import pallas as pl
from jax.experimental.pallas import tpu as pltpu

F32 = jnp.float32
BF16 = jnp.bfloat16

D_MODEL = 1024
CHUNK = 64
M_HEADS = 4
M_HDIM = 256
M_WIDTH = 1024
A_HEADS = 8
A_HDIM = 64
A_WIDTH = 512
PAST_CHUNKS = 8
ATT_REACH = 512
REL_CLIP = 128
R_HEADS = 8
N_KEYS = 128
N_EXPERTS = N_KEYS * N_KEYS
HALF_KEY = 128
TOPK = 16
PLE_DIM = 256
DEPTH = 2
ALPHA = (2 * DEPTH) ** 0.25
LN_EPS = 1e-5

LANES = 128
ATT_Q = 2 * CHUNK
ATT_WIN = ATT_REACH + ATT_Q
VMEM_LIMIT = 48 * 1024 * 1024
NEG_INF = float("-inf")


def _cparams(sem):
    return pltpu.CompilerParams(dimension_semantics=sem, vmem_limit_bytes=VMEM_LIMIT)


def _sigmoid(x):
    return 1.0 / (1.0 + jnp.exp(-x))


def _dot(a, b):
    return jnp.dot(a, b, preferred_element_type=F32)


def _dot_nt(a, b):
    return lax.dot_general(a, b, (((1,), (1,)), ((), ())), preferred_element_type=F32)


def _dot_tn(a, b):
    return lax.dot_general(a, b, (((0,), (0,)), ((), ())), preferred_element_type=F32)


def _layer_norm(y, g, b):
    mu = jnp.mean(y, axis=-1, keepdims=True)
    d = y - mu
    var = jnp.mean(d * d, axis=-1, keepdims=True)
    return d * lax.rsqrt(var + LN_EPS) * g + b


def _proj_body(x_ref, w_ref, s_ref, *o_refs, act):
    acc = _dot(x_ref[...].astype(BF16), w_ref[...]) * s_ref[...]
    if act == "sigmoid":
        acc = _sigmoid(acc)
    for o in o_refs:
        o[...] = acc.astype(o.dtype)


def _proj(x, w, scale, act, out_dtypes, tb=512, nb=512):
    t, k = x.shape
    n = w.shape[1]
    return pl.pallas_call(
        functools.partial(_proj_body, act=act),
        grid=(t // tb, n // nb),
        in_specs=[pl.BlockSpec((tb, k), lambda i, j: (i, 0)),
                  pl.BlockSpec((k, nb), lambda i, j: (0, j)),
                  pl.BlockSpec((1, nb), lambda i, j: (0, j))],
        out_specs=[pl.BlockSpec((tb, nb), lambda i, j: (i, j)) for _ in out_dtypes],
        out_shape=[jax.ShapeDtypeStruct((t, n), d) for d in out_dtypes],
        compiler_params=_cparams(("parallel", "arbitrary")),
    )(x, w, scale)


def _split3(v):
    hi = v.astype(BF16)
    r = v - hi.astype(F32)
    mid = r.astype(BF16)
    lo = (r - mid.astype(F32)).astype(BF16)
    return hi, mid, lo


def _gate_body(x_ref, w_ref, b_ref, o_ref):
    xh, xm, xl = _split3(x_ref[...])
    wh, wm, wl = _split3(w_ref[...])
    z = (_dot(xh, wh) + (_dot(xh, wm) + _dot(xm, wh))
         + (_dot(xh, wl) + _dot(xm, wm) + _dot(xl, wh))) + b_ref[...]
    col = lax.broadcasted_iota(jnp.int32, z.shape, 1)
    log_sig = jnp.minimum(z, 0.0) - jnp.log1p(jnp.exp(-jnp.abs(z)))
    o_ref[...] = jnp.where(col < M_HEADS, z, log_sig)


def _gates(x, w_if, b_if, tb=512):
    t, k = x.shape
    return pl.pallas_call(
        _gate_body,
        grid=(t // tb,),
        in_specs=[pl.BlockSpec((tb, k), lambda i: (i, 0)),
                  pl.BlockSpec((k, LANES), lambda i: (0, 0)),
                  pl.BlockSpec((1, LANES), lambda i: (0, 0))],
        out_specs=pl.BlockSpec((tb, LANES), lambda i: (i, 0)),
        out_shape=jax.ShapeDtypeStruct((t, LANES), F32),
        compiler_params=_cparams(("parallel",)),
    )(x, w_if, b_if)


def _mlstm_body(qkv_ref, g_ref, og_ref, nw_ref, c0_ref, n0_ref, m0_ref,
                h_ref, c_ref, n_ref, m_ref):
    @pl.when(pl.program_id(1) == 0)
    def _():
        c_ref[...] = c0_ref[...]
        n_ref[...] = n0_ref[...]
        m_ref[...] = m0_ref[...]

    L = CHUNK
    g = g_ref[0]
    gt = jnp.concatenate([g, jnp.zeros((LANES - L, LANES), F32)], axis=0).T
    t_i = lax.broadcasted_iota(jnp.int32, (L, L), 0)
    s_i = lax.broadcasted_iota(jnp.int32, (L, L), 1)
    causal = s_i <= t_i
    lane = lax.broadcasted_iota(jnp.int32, (1, LANES), 1)
    m_all = m_ref[0]
    m_next = m_all
    for h in range(M_HEADS):
        sl = slice(h * M_HDIM, (h + 1) * M_HDIM)
        q = qkv_ref[0, :, h * M_HDIM:(h + 1) * M_HDIM]
        k = qkv_ref[0, :, M_WIDTH + h * M_HDIM:M_WIDTH + (h + 1) * M_HDIM]
        v = qkv_ref[0, :, 2 * M_WIDTH + h * M_HDIM:2 * M_WIDTH + (h + 1) * M_HDIM]
        ig_col = g[:, h:h + 1]
        lf_col = g[:, M_HEADS + h:M_HEADS + h + 1]
        ig_row = gt[h:h + 1, :L]
        lf_row = gt[M_HEADS + h:M_HEADS + h + 1, :L]
        b_col = jnp.sum(jnp.where(causal, lf_row, 0.0), axis=1, keepdims=True)
        b_row = jnp.sum(jnp.where(t_i <= s_i, lf_col, 0.0), axis=0, keepdims=True)
        m_prev = jnp.sum(jnp.where(lane == h, m_all, 0.0), axis=1, keepdims=True)
        inter = b_col + m_prev
        dmat = jnp.where(causal, b_col - b_row + ig_row, NEG_INF)
        m_t = jnp.maximum(inter, jnp.max(dmat, axis=1, keepdims=True))
        w_inter = jnp.exp(inter - m_t)
        w_intra = jnp.exp(dmat - m_t)
        a = w_intra * _dot_nt(q, k)
        c_h = c_ref[0, h]
        n_h = n_ref[0, :, sl]
        qf = q.astype(F32)
        num = w_inter * _dot_nt(q, c_h.astype(BF16)) + _dot(a.astype(BF16), v)
        den = w_inter * jnp.sum(qf * n_h, axis=1, keepdims=True) + jnp.sum(a, axis=1, keepdims=True)
        hh = num * (1.0 / jnp.maximum(jnp.abs(den), jnp.exp(-m_t)))
        mu = jnp.mean(hh, axis=1, keepdims=True)
        dv = hh - mu
        var = jnp.mean(dv * dv, axis=1, keepdims=True)
        hn = dv * lax.rsqrt(var + LN_EPS) * nw_ref[:, sl]
        h_ref[0, :, sl] = (og_ref[0, :, sl].astype(F32) * hn).astype(h_ref.dtype)
        m_new = m_t[L - 1:L, :]
        b_last = b_col[L - 1:L, :]
        g_state = jnp.exp(b_last + m_prev - m_new)
        g_s = jnp.exp(b_last - b_col + ig_col - m_new)
        vg = (g_s * v.astype(F32)).astype(BF16)
        c_ref[0, h] = g_state * c_h + _dot_tn(vg, k)
        n_ref[0, :, sl] = g_state * n_h + jnp.sum(g_s * k.astype(F32), axis=0, keepdims=True)
        m_next = jnp.where(lane == h, m_new, m_next)
    m_ref[0] = m_next


def _mlstm(qkv, gates, og3, norm_w, c0, n0, m0):
    b, s, _ = qkv.shape
    nc = s // CHUNK
    return pl.pallas_call(
        _mlstm_body,
        grid=(b, nc),
        in_specs=[pl.BlockSpec((1, CHUNK, 3 * M_WIDTH), lambda i, c: (i, c, 0)),
                  pl.BlockSpec((1, CHUNK, LANES), lambda i, c: (i, c, 0)),
                  pl.BlockSpec((1, CHUNK, M_WIDTH), lambda i, c: (i, c, 0)),
                  pl.BlockSpec((1, M_WIDTH), lambda i, c: (0, 0)),
                  pl.BlockSpec((1, M_HEADS, M_HDIM, M_HDIM), lambda i, c: (i, 0, 0, 0)),
                  pl.BlockSpec((1, 1, M_WIDTH), lambda i, c: (i, 0, 0)),
                  pl.BlockSpec((1, 1, LANES), lambda i, c: (i, 0, 0))],
        out_specs=[pl.BlockSpec((1, CHUNK, M_WIDTH), lambda i, c: (i, c, 0)),
                   pl.BlockSpec((1, M_HEADS, M_HDIM, M_HDIM), lambda i, c: (i, 0, 0, 0)),
                   pl.BlockSpec((1, 1, M_WIDTH), lambda i, c: (i, 0, 0)),
                   pl.BlockSpec((1, 1, LANES), lambda i, c: (i, 0, 0))],
        out_shape=[jax.ShapeDtypeStruct((b, s, M_WIDTH), BF16),
                   jax.ShapeDtypeStruct((b, M_HEADS, M_HDIM, M_HDIM), F32),
                   jax.ShapeDtypeStruct((b, 1, M_WIDTH), F32),
                   jax.ShapeDtypeStruct((b, 1, LANES), F32)],
        compiler_params=_cparams(("parallel", "arbitrary")),
    )(qkv, gates, og3, norm_w, c0, n0, m0)


def _attn_body(q_ref, k_ref, v_ref, bias_ref, o_ref, *, first_valid_block):
    p = pl.program_id(1)
    row0 = pl.multiple_of(p * ATT_Q, ATT_Q)
    kw = k_ref[0, pl.ds(row0, ATT_WIN), :]
    vw = v_ref[0, pl.ds(row0, ATT_WIN), :]
    key_j = lax.broadcasted_iota(jnp.int32, (ATT_Q, ATT_WIN), 1)
    key_ok = key_j >= (first_valid_block - p) * ATT_Q
    lane = lax.broadcasted_iota(jnp.int32, (1, LANES), 1)
    for pair in range(A_HEADS // 2):
        cs = slice(pair * LANES, (pair + 1) * LANES)
        qp = q_ref[0, :, cs]
        kp = kw[:, cs]
        vp = vw[:, cs]
        o_pair = jnp.zeros((ATT_Q, LANES), F32)
        for e in range(2):
            head = 2 * pair + e
            in_head = (lane >= e * A_HDIM) & (lane < (e + 1) * A_HDIM)
            qm = jnp.where(in_head, qp, jnp.zeros_like(qp))
            s = _dot_nt(qm, kp) * (A_HDIM ** -0.5) + bias_ref[head]
            s = jnp.where(key_ok, s, NEG_INF)
            m = jnp.max(s, axis=1, keepdims=True)
            pe = jnp.exp(s - m)
            l = jnp.sum(pe, axis=1, keepdims=True)
            vm = jnp.where(in_head, vp, jnp.zeros_like(vp))
            o_pair = o_pair + _dot(pe.astype(BF16), vm) * (1.0 / l)
        o_ref[0, :, cs] = o_pair.astype(o_ref.dtype)


def _attn(q, k_pad, v_pad, bias, first_valid_block):
    b, sq, _ = q.shape
    sk = k_pad.shape[1]
    return pl.pallas_call(
        functools.partial(_attn_body, first_valid_block=first_valid_block),
        grid=(b, sq // ATT_Q),
        in_specs=[pl.BlockSpec((1, ATT_Q, A_WIDTH), lambda i, p: (i, p, 0)),
                  pl.BlockSpec((1, sk, A_WIDTH), lambda i, p: (i, 0, 0)),
                  pl.BlockSpec((1, sk, A_WIDTH), lambda i, p: (i, 0, 0)),
                  pl.BlockSpec((A_HEADS, ATT_Q, ATT_WIN), lambda i, p: (0, 0, 0))],
        out_specs=pl.BlockSpec((1, ATT_Q, A_WIDTH), lambda i, p: (i, p, 0)),
        out_shape=jax.ShapeDtypeStruct((b, sq, A_WIDTH), BF16),
        compiler_params=_cparams(("parallel", "arbitrary")),
    )(q, k_pad, v_pad, bias)


def _attn_bias(rel_bias):
    i = jnp.arange(ATT_Q)[:, None]
    j = jnp.arange(ATT_WIN)[None, :]
    rel = jnp.clip(ATT_REACH + i - j, -REL_CLIP, REL_CLIP) + REL_CLIP
    qc = i // CHUNK
    kc = j // CHUNK - PAST_CHUNKS
    visible = (kc <= qc) & (kc >= qc - PAST_CHUNKS)
    return jnp.where(visible[None], rel_bias.astype(F32)[:, rel], NEG_INF)


def _mix_body(hg_ref, oa_ref, gm_ref, ga_ref, x_ref, wbm_ref, wba_ref, wo_ref, g_ref, b_ref, o_ref):
    ym = _dot(hg_ref[...], wbm_ref[...])
    ya = _dot(oa_ref[...], wba_ref[...])
    gated = gm_ref[...].astype(F32) * ym + ga_ref[...].astype(F32) * ya
    mix = _dot(gated.astype(BF16), wo_ref[...])
    o_ref[...] = _layer_norm(ALPHA * x_ref[...] + mix, g_ref[...], b_ref[...])


def _mix(hg, oa, sg3, x, wbm, wba, wo, ln_g, ln_b, tb=512):
    t = x.shape[0]
    row = lambda i: (i, 0)
    const = lambda i: (0, 0)
    return pl.pallas_call(
        _mix_body,
        grid=(t // tb,),
        in_specs=[pl.BlockSpec((tb, M_WIDTH), row),
                  pl.BlockSpec((tb, A_WIDTH), row),
                  pl.BlockSpec((tb, D_MODEL), lambda i: (i, 1)),
                  pl.BlockSpec((tb, D_MODEL), lambda i: (i, 2)),
                  pl.BlockSpec((tb, D_MODEL), row),
                  pl.BlockSpec((M_WIDTH, D_MODEL), const),
                  pl.BlockSpec((A_WIDTH, D_MODEL), const),
                  pl.BlockSpec((D_MODEL, D_MODEL), const),
                  pl.BlockSpec((1, D_MODEL), const),
                  pl.BlockSpec((1, D_MODEL), const)],
        out_specs=pl.BlockSpec((tb, D_MODEL), row),
        out_shape=jax.ShapeDtypeStruct((t, D_MODEL), F32),
        compiler_params=_cparams(("parallel",)),
    )(hg, oa, sg3, sg3, x, wbm, wba, wo, ln_g, ln_b)


_CAND_GROUPS = [(a, 0) for a in range(8)] + [(0, 8), (8, 0)]


def _top16(work, key, n_extract=TOPK):
    big = jnp.int32(1 << 30)
    rank = lax.broadcasted_iota(jnp.int32, (n_extract, work.shape[1]), 0)
    vals = jnp.zeros((n_extract, work.shape[1]), F32)
    for r in range(n_extract):
        mx = jnp.max(work, axis=0, keepdims=True)
        first = jnp.min(jnp.where(work == mx, key, big), axis=0, keepdims=True)
        work = jnp.where(key == first, NEG_INF, work)
        vals = jnp.where(rank == r, mx, vals)
    return vals


def _peer_sel_body(x_ref, wq_ref, sk_ref, s_ref, aux_ref):
    tb = x_ref.shape[0]
    qt = _dot_nt(wq_ref[...], x_ref[...].astype(BF16))
    key_row = lax.broadcasted_iota(jnp.int32, (N_KEYS, tb), 0)
    sub8 = lax.broadcasted_iota(jnp.int32, (8, tb), 0)
    cand_key = jnp.concatenate(
        [(a * TOPK + b0) + sub8 for (a, b0) in _CAND_GROUPS[:9]] + [8 * TOPK + sub8 * TOPK], axis=0)
    for h in range(R_HEADS):
        sv = []
        for p in range(2):
            r = h * 2 + p
            qhp = qt[r * HALF_KEY:(r + 1) * HALF_KEY, :].astype(BF16)
            st = _dot(sk_ref[r], qhp)
            s_ref[r] = st
            sv.append(_top16(st, key_row))
        sv1, sv2 = sv
        groups = [sv1[a:a + 1, :] + sv2[0:8, :] for a in range(8)]
        groups.append(sv1[0:1, :] + sv2[8:16, :])
        groups.append(sv1[8:16, :] + sv2[0:1, :])
        cv = _top16(jnp.concatenate(groups, axis=0), cand_key)
        top = cv[0:1, :]
        z = jnp.sum(jnp.exp(cv - top), axis=0, keepdims=True)
        aux_ref[h:h + 1, :] = cv[TOPK - 1:TOPK, :]
        aux_ref[R_HEADS + h:R_HEADS + h + 1, :] = top + jnp.log(z)


def _peer_sel(x, wq_t, subkeys, tb=256):
    t = x.shape[0]
    return pl.pallas_call(
        _peer_sel_body,
        grid=(t // tb,),
        in_specs=[pl.BlockSpec((tb, D_MODEL), lambda i: (i, 0)),
                  pl.BlockSpec((2 * R_HEADS * HALF_KEY, D_MODEL), lambda i: (0, 0)),
                  pl.BlockSpec((2 * R_HEADS, N_KEYS, HALF_KEY), lambda i: (0, 0, 0))],
        out_specs=[pl.BlockSpec((2 * R_HEADS, N_KEYS, tb), lambda i: (0, 0, i)),
                   pl.BlockSpec((2 * R_HEADS, tb), lambda i: (0, i))],
        out_shape=[jax.ShapeDtypeStruct((2 * R_HEADS, N_KEYS, t), F32),
                   jax.ShapeDtypeStruct((2 * R_HEADS, t), F32)],
        compiler_params=_cparams(("parallel",)),
    )(x, wq_t, subkeys)


def _gelu_tanh(x):
    return 0.5 * x * (1.0 + jnp.tanh(math.sqrt(2.0 / math.pi) * (x + 0.044715 * (x * x * x))))


def _peer_ffn_body(x_ref, u_ref, vt_ref, s_ref, aux_ref, pe_ref, plp_ref, plg_ref, g_ref, b_ref,
                   o_ref, xb_scr, acc_scr, p_scr, *, eb):
    e = pl.program_id(1)
    tb = x_ref.shape[0]

    @pl.when(e == 0)
    def _():
        xb_scr[...] = x_ref[...].astype(BF16)
        acc_scr[...] = jnp.zeros_like(acc_scr)

    act = _gelu_tanh(_dot_nt(u_ref[...], xb_scr[...]))
    rows_per_step = eb // N_KEYS
    assert rows_per_step == 8, "one aligned sublane group of first-half scores per step"
    i0 = pl.multiple_of(e * rows_per_step, rows_per_step)
    for ii in range(rows_per_step):
        for lg in range(tb // LANES):
            ls = slice(lg * LANES, (lg + 1) * LANES)
            w = jnp.zeros((N_KEYS, LANES), F32)
            for h in range(R_HEADS):
                s1 = s_ref[2 * h, pl.ds(i0, rows_per_step), ls][ii:ii + 1, :]
                s = s1 + s_ref[2 * h + 1, :, ls]
                tau = aux_ref[h:h + 1, ls]
                c = aux_ref[R_HEADS + h:R_HEADS + h + 1, ls]
                w = w + jnp.where(s >= tau, jnp.exp(s - c), 0.0)
            p_scr[ii * N_KEYS:(ii + 1) * N_KEYS, ls] = (act[ii * N_KEYS:(ii + 1) * N_KEYS, ls] * w).astype(BF16)
    acc_scr[...] += _dot(vt_ref[...], p_scr[...])

    @pl.when(e == pl.num_programs(1) - 1)
    def _():
        x = x_ref[...]
        y = _layer_norm(ALPHA * x + acc_scr[...].T, g_ref[...], b_ref[...])
        ple = _dot(pe_ref[...].astype(BF16), plp_ref[...])
        gate = _sigmoid(_dot(y.astype(BF16), plg_ref[...]))
        o_ref[...] = y + ple * gate


def _peer_ffn(x, u, vt, scores, aux, pe, plp, plg, ln_g, ln_b, tb=512, eb=1024):
    t = x.shape[0]
    row = lambda i, e: (i, 0)
    const = lambda i, e: (0, 0)
    return pl.pallas_call(
        functools.partial(_peer_ffn_body, eb=eb),
        grid=(t // tb, N_EXPERTS // eb),
        in_specs=[pl.BlockSpec((tb, D_MODEL), row),
                  pl.BlockSpec((eb, D_MODEL), lambda i, e: (e, 0)),
                  pl.BlockSpec((D_MODEL, eb), lambda i, e: (0, e)),
                  pl.BlockSpec((2 * R_HEADS, N_KEYS, tb), lambda i, e: (0, 0, i)),
                  pl.BlockSpec((2 * R_HEADS, tb), lambda i, e: (0, i)),
                  pl.BlockSpec((tb, PLE_DIM), row),
                  pl.BlockSpec((PLE_DIM, D_MODEL), const),
                  pl.BlockSpec((D_MODEL, D_MODEL), const),
                  pl.BlockSpec((1, D_MODEL), const),
                  pl.BlockSpec((1, D_MODEL), const)],
        out_specs=pl.BlockSpec((tb, D_MODEL), row),
        out_shape=jax.ShapeDtypeStruct((t, D_MODEL), F32),
        scratch_shapes=[pltpu.VMEM((tb, D_MODEL), BF16),
                        pltpu.VMEM((D_MODEL, tb), F32),
                        pltpu.VMEM((eb, tb), BF16)],
        compiler_params=_cparams(("parallel", "arbitrary")),
    )(x, u, vt, scores, aux, pe, plp, plg, ln_g, ln_b)


def _prep_layer(w_in, if_bias, norm_w, rel_bias, w_bm, w_ba, w_o, ln1g, ln1b, ln2g, ln2b,
                pq, psk, pu, pv, plp, plg):
    c = 0
    cols = {}
    for name, width in (("mq", M_WIDTH), ("mk", M_WIDTH), ("mv", M_WIDTH), ("mi", M_HEADS), ("mf", M_HEADS),
                        ("mo", M_WIDTH), ("aq", A_WIDTH), ("ak", A_WIDTH), ("av", A_WIDTH),
                        ("gm", D_MODEL), ("ga", D_MODEL)):
        cols[name] = w_in[:, c:c + width]
        c += width
    cat = lambda names: jnp.concatenate([cols[n] for n in names], axis=1)
    ones = lambda n: jnp.ones((1, n), F32)
    w_if = jnp.pad(cat(("mi", "mf")), ((0, 0), (0, LANES - 2 * M_HEADS)))
    b_if = jnp.pad(if_bias.reshape(1, 2 * M_HEADS).astype(F32), ((0, 0), (0, LANES - 2 * M_HEADS)))
    return dict(
        w_mqkv=cat(("mq", "mk", "mv")).astype(BF16),
        s_mqkv=jnp.concatenate([ones(M_WIDTH), ones(M_WIDTH) * (M_HDIM ** -0.5), ones(M_WIDTH)], axis=1),
        w_sig=cat(("mo", "gm", "ga")).astype(BF16), s_sig=ones(3 * D_MODEL),
        w_aqkv=cat(("aq", "ak", "av")).astype(BF16), s_aqkv=ones(3 * A_WIDTH),
        w_if=w_if, b_if=b_if,
        norm_w=norm_w.reshape(1, M_WIDTH).astype(F32),
        bias=_attn_bias(rel_bias),
        w_bm=w_bm.astype(BF16), w_ba=w_ba.astype(BF16), w_o=w_o.astype(BF16),
        ln1g=ln1g.reshape(1, D_MODEL), ln1b=ln1b.reshape(1, D_MODEL),
        ln2g=ln2g.reshape(1, D_MODEL), ln2b=ln2b.reshape(1, D_MODEL),
        wq_t=pq.T.astype(BF16),
        subkeys=psk.reshape(2 * R_HEADS, N_KEYS, HALF_KEY).astype(BF16),
        u=pu.astype(BF16), vt=pv.T.astype(BF16),
        plp=plp.astype(BF16), plg=plg.astype(BF16),
    )


def _layer(x3, pe3, lw, state, cache):
    b, s, _ = x3.shape
    t = b * s
    x = x3.reshape(t, D_MODEL)
    mqkv, = _proj(x, lw["w_mqkv"], lw["s_mqkv"], None, (BF16,))
    sig3, = _proj(x, lw["w_sig"], lw["s_sig"], "sigmoid", (BF16,))
    aqkv, akv32 = _proj(x, lw["w_aqkv"], lw["s_aqkv"], None, (BF16, F32))
    gates = _gates(x, lw["w_if"], lw["b_if"])

    if state is None:
        c0 = jnp.zeros((b, M_HEADS, M_HDIM, M_HDIM), F32)
        n0 = jnp.zeros((b, 1, M_WIDTH), F32)
        m0 = jnp.zeros((b, 1, LANES), F32)
    else:
        c0 = state[0].astype(F32)
        n0 = state[1].astype(F32).reshape(b, 1, M_WIDTH)
        m0 = jnp.pad(state[2].astype(F32), ((0, 0), (0, LANES - M_HEADS))).reshape(b, 1, LANES)
    hg, c_new, n_new, m_new = _mlstm(mqkv.reshape(b, s, 3 * M_WIDTH), gates.reshape(b, s, LANES),
                                     sig3.reshape(b, s, 3 * D_MODEL), lw["norm_w"], c0, n0, m0)
    n_new = n_new.reshape(b, M_HEADS, M_HDIM)
    m_new = m_new.reshape(b, LANES)[:, :M_HEADS]

    a3 = aqkv.reshape(b, s, 3 * A_WIDTH)
    q, k, v = a3[..., :A_WIDTH], a3[..., A_WIDTH:2 * A_WIDTH], a3[..., 2 * A_WIDTH:]
    kv32 = akv32.reshape(b, s, 3 * A_WIDTH)
    if cache is None:
        hist_k = jnp.zeros((b, ATT_REACH, A_WIDTH), BF16)
        hist_v = hist_k
        first_valid_block = ATT_REACH // ATT_Q
        keep = min(ATT_REACH, s)
        k_out = kv32[:, s - keep:, A_WIDTH:2 * A_WIDTH].reshape(b, keep, A_HEADS, A_HDIM)
        v_out = kv32[:, s - keep:, 2 * A_WIDTH:].reshape(b, keep, A_HEADS, A_HDIM)
    else:
        hist_k = cache[0].reshape(b, ATT_REACH, A_WIDTH).astype(BF16)
        hist_v = cache[1].reshape(b, ATT_REACH, A_WIDTH).astype(BF16)
        first_valid_block = 0
        k_out = kv32[..., A_WIDTH:2 * A_WIDTH].reshape(b, s, A_HEADS, A_HDIM)
        v_out = kv32[..., 2 * A_WIDTH:].reshape(b, s, A_HEADS, A_HDIM)
    sq = -(-s // ATT_Q) * ATT_Q
    tail = ((0, 0), (0, sq - s), (0, 0))
    oa = _attn(jnp.pad(q, tail), jnp.pad(jnp.concatenate([hist_k, k], axis=1), tail),
               jnp.pad(jnp.concatenate([hist_v, v], axis=1), tail), lw["bias"], first_valid_block)
    oa = oa[:, :s].reshape(t, A_WIDTH)

    x1 = _mix(hg.reshape(t, M_WIDTH), oa, sig3, x, lw["w_bm"], lw["w_ba"], lw["w_o"], lw["ln1g"], lw["ln1b"])
    scores, aux = _peer_sel(x1, lw["wq_t"], lw["subkeys"])
    x3_out = _peer_ffn(x1, lw["u"], lw["vt"], scores, aux, pe3.reshape(t, PLE_DIM), lw["plp"], lw["plg"],
                       lw["ln2g"], lw["ln2b"])
    return x3_out.reshape(b, s, D_MODEL), (c_new, n_new, m_new), (k_out, v_out)


def kernel(x_prompt, x_sample, cache_attn_k, cache_attn_v, state_mlstm_C, state_mlstm_n, state_mlstm_m,
           p_prompt, p_sample, w_in, mlstm_if_bias, mlstm_norm_w, attn_rel_bias, w_branch_m, w_branch_a,
           w_out, ln1_g, ln1_b, ln2_g, ln2_b, peer_wq, peer_subkeys, peer_u, peer_v, ple_proj, ple_gate):
    layer_w = (w_in, mlstm_if_bias, mlstm_norm_w, attn_rel_bias, w_branch_m, w_branch_a, w_out,
               ln1_g, ln1_b, ln2_g, ln2_b, peer_wq, peer_subkeys, peer_u, peer_v, ple_proj, ple_gate)
    yp, ys = x_prompt, x_sample
    outs_p = [[] for _ in range(5)]
    outs_s = [[] for _ in range(5)]
    for i in range(DEPTH):
        lw = _prep_layer(*(w[i] for w in layer_w))
        yp, (cp, np_, mp), (kp, vp) = _layer(yp, p_prompt[i], lw, None, None)
        ys, (cs, ns, ms), (ks, vs) = _layer(
            ys, p_sample[i], lw, (state_mlstm_C[i], state_mlstm_n[i], state_mlstm_m[i]),
            (cache_attn_k[i], cache_attn_v[i]))
        for lst, val in zip(outs_p, (kp, vp, cp, np_, mp)):
            lst.append(val)
        for lst, val in zip(outs_s, (ks, vs, cs, ns, ms)):
            lst.append(val)
    st = lambda l: jnp.stack(l, axis=0)
    return (yp, ys) + tuple(st(l) for l in outs_p) + tuple(st(l) for l in outs_s)
```

```python
import functools
import math

import jax
import jax.numpy as jnp
from jax import lax
from jax.experimental import pallas as pl
from jax.experimental.pallas import tpu as pltpu

F32 = jnp.float32
BF16 = jnp.bfloat16

D_MODEL = 1024
CHUNK = 64
M_HEADS = 4
M_HDIM = 256
M_WIDTH = 1024
A_HEADS = 8
A_HDIM = 64
A_WIDTH = 512
PAST_CHUNKS = 8
ATT_REACH = 512
REL_CLIP = 128
R_HEADS = 8
N_KEYS = 128
N_EXPERTS = N_KEYS * N_KEYS
HALF_KEY = 128
TOPK = 16
PLE_DIM = 256
DEPTH = 2
ALPHA = (2 * DEPTH) ** 0.25
LN_EPS = 1e-5

LANES = 128
ATT_Q = 2 * CHUNK
ATT_WIN = ATT_REACH + ATT_Q
VMEM_LIMIT = 48 * 1024 * 1024
NEG_INF = float("-inf")


def _cparams(sem):
    return pltpu.CompilerParams(dimension_semantics=sem, vmem_limit_bytes=VMEM_LIMIT)


def _sigmoid(x):
    return 1.0 / (1.0 + jnp.exp(-x))


def _dot(a, b):
    return jnp.dot(a, b, preferred_element_type=F32)


def _dot_nt(a, b):
    return lax.dot_general(a, b, (((1,), (1,)), ((), ())), preferred_element_type=F32)


def _dot_tn(a, b):
    return lax.dot_general(a, b, (((0,), (0,)), ((), ())), preferred_element_type=F32)


def _layer_norm(y, g, b):
    mu = jnp.mean(y, axis=-1, keepdims=True)
    d = y - mu
    var = jnp.mean(d * d, axis=-1, keepdims=True)
    return d * lax.rsqrt(var + LN_EPS) * g + b


def _proj_body(x_ref, w_ref, s_ref, *o_refs, act):
    acc = _dot(x_ref[...].astype(BF16), w_ref[...]) * s_ref[...]
    if act == "sigmoid":
        acc = _sigmoid(acc)
    for o in o_refs:
        o[...] = acc.astype(o.dtype)


def _proj(x, w, scale, act, out_dtypes, tb=512, nb=512):
    t, k = x.shape
    n = w.shape[1]
    return pl.pallas_call(
        functools.partial(_proj_body, act=act),
        grid=(t // tb, n // nb),
        in_specs=[pl.BlockSpec((tb, k), lambda i, j: (i, 0)),
                  pl.BlockSpec((k, nb), lambda i, j: (0, j)),
                  pl.BlockSpec((1, nb), lambda i, j: (0, j))],
        out_specs=[pl.BlockSpec((tb, nb), lambda i, j: (i, j)) for _ in out_dtypes],
        out_shape=[jax.ShapeDtypeStruct((t, n), d) for d in out_dtypes],
        compiler_params=_cparams(("parallel", "arbitrary")),
    )(x, w, scale)


def _split3(v):
    hi = v.astype(BF16)
    r = v - hi.astype(F32)
    mid = r.astype(BF16)
    lo = (r - mid.astype(F32)).astype(BF16)
    return hi, mid, lo


def _gate_body(x_ref, w_ref, b_ref, o_ref):
    xh, xm, xl = _split3(x_ref[...])
    wh, wm, wl = _split3(w_ref[...])
    z = (_dot(xh, wh) + (_dot(xh, wm) + _dot(xm, wh))
         + (_dot(xh, wl) + _dot(xm, wm) + _dot(xl, wh))) + b_ref[...]
    col = lax.broadcasted_iota(jnp.int32, z.shape, 1)
    log_sig = jnp.minimum(z, 0.0) - jnp.log1p(jnp.exp(-jnp.abs(z)))
    o_ref[...] = jnp.where(col < M_HEADS, z, log_sig)


def _gates(x, w_if, b_if, tb=512):
    t, k = x.shape
    return pl.pallas_call(
        _gate_body,
        grid=(t // tb,),
        in_specs=[pl.BlockSpec((tb, k), lambda i: (i, 0)),
                  pl.BlockSpec((k, LANES), lambda i: (0, 0)),
                  pl.BlockSpec((1, LANES), lambda i: (0, 0))],
        out_specs=pl.BlockSpec((tb, LANES), lambda i: (i, 0)),
        out_shape=jax.ShapeDtypeStruct((t, LANES), F32),
        compiler_params=_cparams(("parallel",)),
    )(x, w_if, b_if)


def _mlstm_body(qkv_ref, g_ref, og_ref, nw_ref, c0_ref, n0_ref, m0_ref,
                h_ref, c_ref, n_ref, m_ref):
    @pl.when(pl.program_id(1) == 0)
    def _():
        c_ref[...] = c0_ref[...]
        n_ref[...] = n0_ref[...]
        m_ref[...] = m0_ref[...]

    L = CHUNK
    g = g_ref[0]
    gt = jnp.concatenate([g, jnp.zeros((LANES - L, LANES), F32)], axis=0).T
    t_i = lax.broadcasted_iota(jnp.int32, (L, L), 0)
    s_i = lax.broadcasted_iota(jnp.int32, (L, L), 1)
    causal = s_i <= t_i
    lane = lax.broadcasted_iota(jnp.int32, (1, LANES), 1)
    m_all = m_ref[0]
    m_next = m_all
    for h in range(M_HEADS):
        sl = slice(h * M_HDIM, (h + 1) * M_HDIM)
        q = qkv_ref[0, :, h * M_HDIM:(h + 1) * M_HDIM]
        k = qkv_ref[0, :, M_WIDTH + h * M_HDIM:M_WIDTH + (h + 1) * M_HDIM]
        v = qkv_ref[0, :, 2 * M_WIDTH + h * M_HDIM:2 * M_WIDTH + (h + 1) * M_HDIM]
        ig_col = g[:, h:h + 1]
        lf_col = g[:, M_HEADS + h:M_HEADS + h + 1]
        ig_row = gt[h:h + 1, :L]
        lf_row = gt[M_HEADS + h:M_HEADS + h + 1, :L]
        b_col = jnp.sum(jnp.where(causal, lf_row, 0.0), axis=1, keepdims=True)
        b_row = jnp.sum(jnp.where(t_i <= s_i, lf_col, 0.0), axis=0, keepdims=True)
        m_prev = jnp.sum(jnp.where(lane == h, m_all, 0.0), axis=1, keepdims=True)
        inter = b_col + m_prev
        dmat = jnp.where(causal, b_col - b_row + ig_row, NEG_INF)
        m_t = jnp.maximum(inter, jnp.max(dmat, axis=1, keepdims=True))
        w_inter = jnp.exp(inter - m_t)
        w_intra = jnp.exp(dmat - m_t)
        a = w_intra * _dot_nt(q, k)
        c_h = c_ref[0, h]
        n_h = n_ref[0, :, sl]
        qf = q.astype(F32)
        num = w_inter * _dot_nt(q, c_h.astype(BF16)) + _dot(a.astype(BF16), v)
        den = w_inter * jnp.sum(qf * n_h, axis=1, keepdims=True) + jnp.sum(a, axis=1, keepdims=True)
        hh = num * (1.0 / jnp.maximum(jnp.abs(den), jnp.exp(-m_t)))
        mu = jnp.mean(hh, axis=1, keepdims=True)
        dv = hh - mu
        var = jnp.mean(dv * dv, axis=1, keepdims=True)
        hn = dv * lax.rsqrt(var + LN_EPS) * nw_ref[:, sl]
        h_ref[0, :, sl] = (og_ref[0, :, sl].astype(F32) * hn).astype(h_ref.dtype)
        m_new = m_t[L - 1:L, :]
        b_last = b_col[L - 1:L, :]
        g_state = jnp.exp(b_last + m_prev - m_new)
        g_s = jnp.exp(b_last - b_col + ig_col - m_new)
        vg = (g_s * v.astype(F32)).astype(BF16)
        c_ref[0, h] = g_state * c_h + _dot_tn(vg, k)
        n_ref[0, :, sl] = g_state * n_h + jnp.sum(g_s * k.astype(F32), axis=0, keepdims=True)
        m_next = jnp.where(lane == h, m_new, m_next)
    m_ref[0] = m_next


def _mlstm(qkv, gates, og3, norm_w, c0, n0, m0):
    b, s, _ = qkv.shape
    nc = s // CHUNK
    return pl.pallas_call(
        _mlstm_body,
        grid=(b, nc),
        in_specs=[pl.BlockSpec((1, CHUNK, 3 * M_WIDTH), lambda i, c: (i, c, 0)),
                  pl.BlockSpec((1, CHUNK, LANES), lambda i, c: (i, c, 0)),
                  pl.BlockSpec((1, CHUNK, M_WIDTH), lambda i, c: (i, c, 0)),
                  pl.BlockSpec((1, M_WIDTH), lambda i, c: (0, 0)),
                  pl.BlockSpec((1, M_HEADS, M_HDIM, M_HDIM), lambda i, c: (i, 0, 0, 0)),
                  pl.BlockSpec((1, 1, M_WIDTH), lambda i, c: (i, 0, 0)),
                  pl.BlockSpec((1, 1, LANES), lambda i, c: (i, 0, 0))],
        out_specs=[pl.BlockSpec((1, CHUNK, M_WIDTH), lambda i, c: (i, c, 0)),
                   pl.BlockSpec((1, M_HEADS, M_HDIM, M_HDIM), lambda i, c: (i, 0, 0, 0)),
                   pl.BlockSpec((1, 1, M_WIDTH), lambda i, c: (i, 0, 0)),
                   pl.BlockSpec((1, 1, LANES), lambda i, c: (i, 0, 0))],
        out_shape=[jax.ShapeDtypeStruct((b, s, M_WIDTH), BF16),
                   jax.ShapeDtypeStruct((b, M_HEADS, M_HDIM, M_HDIM), F32),
                   jax.ShapeDtypeStruct((b, 1, M_WIDTH), F32),
                   jax.ShapeDtypeStruct((b, 1, LANES), F32)],
        compiler_params=_cparams(("parallel", "arbitrary")),
    )(qkv, gates, og3, norm_w, c0, n0, m0)


def _attn_body(q_ref, k_ref, v_ref, bias_ref, o_ref, *, first_valid_block):
    p = pl.program_id(1)
    row0 = pl.multiple_of(p * ATT_Q, ATT_Q)
    kw = k_ref[0, pl.ds(row0, ATT_WIN), :]
    vw = v_ref[0, pl.ds(row0, ATT_WIN), :]
    key_j = lax.broadcasted_iota(jnp.int32, (ATT_Q, ATT_WIN), 1)
    key_ok = key_j >= (first_valid_block - p) * ATT_Q
    lane = lax.broadcasted_iota(jnp.int32, (1, LANES), 1)
    for pair in range(A_HEADS // 2):
        cs = slice(pair * LANES, (pair + 1) * LANES)
        qp = q_ref[0, :, cs]
        kp = kw[:, cs]
        vp = vw[:, cs]
        o_pair = jnp.zeros((ATT_Q, LANES), F32)
        for e in range(2):
            head = 2 * pair + e
            in_head = (lane >= e * A_HDIM) & (lane < (e + 1) * A_HDIM)
            qm = jnp.where(in_head, qp, jnp.zeros_like(qp))
            s = _dot_nt(qm, kp) * (A_HDIM ** -0.5) + bias_ref[head]
            s = jnp.where(key_ok, s, NEG_INF)
            m = jnp.max(s, axis=1, keepdims=True)
            pe = jnp.exp(s - m)
            l = jnp.sum(pe, axis=1, keepdims=True)
            vm = jnp.where(in_head, vp, jnp.zeros_like(vp))
            o_pair = o_pair + _dot(pe.astype(BF16), vm) * (1.0 / l)
        o_ref[0, :, cs] = o_pair.astype(o_ref.dtype)


def _attn(q, k_pad, v_pad, bias, first_valid_block):
    b, sq, _ = q.shape
    sk = k_pad.shape[1]
    return pl.pallas_call(
        functools.partial(_attn_body, first_valid_block=first_valid_block),
        grid=(b, sq // ATT_Q),
        in_specs=[pl.BlockSpec((1, ATT_Q, A_WIDTH), lambda i, p: (i, p, 0)),
                  pl.BlockSpec((1, sk, A_WIDTH), lambda i, p: (i, 0, 0)),
                  pl.BlockSpec((1, sk, A_WIDTH), lambda i, p: (i, 0, 0)),
                  pl.BlockSpec((A_HEADS, ATT_Q, ATT_WIN), lambda i, p: (0, 0, 0))],
        out_specs=pl.BlockSpec((1, ATT_Q, A_WIDTH), lambda i, p: (i, p, 0)),
        out_shape=jax.ShapeDtypeStruct((b, sq, A_WIDTH), BF16),
        compiler_params=_cparams(("parallel", "arbitrary")),
    )(q, k_pad, v_pad, bias)


def _attn_bias(rel_bias):
    i = jnp.arange(ATT_Q)[:, None]
    j = jnp.arange(ATT_WIN)[None, :]
    n_diag = ATT_Q + ATT_WIN
    k = jnp.arange(n_diag)
    e = rel_bias.astype(F32)[:, jnp.clip(ATT_REACH + ATT_Q - 1 - k, -REL_CLIP, REL_CLIP) + REL_CLIP]
    skew = jnp.tile(e, (1, ATT_Q))[:, :ATT_Q * (n_diag - 1)].reshape(A_HEADS, ATT_Q, n_diag - 1)
    bias = skew[:, :, ATT_Q - 1:ATT_Q - 1 + ATT_WIN]
    qc = i // CHUNK
    kc = j // CHUNK - PAST_CHUNKS
    visible = (kc <= qc) & (kc >= qc - PAST_CHUNKS)
    return jnp.where(visible[None], bias, NEG_INF)


def _mix_body(hg_ref, oa_ref, gm_ref, ga_ref, x_ref, wbm_ref, wba_ref, wo_ref, g_ref, b_ref, o_ref):
    ym = _dot(hg_ref[...], wbm_ref[...])
    ya = _dot(oa_ref[...], wba_ref[...])
    gated = gm_ref[...].astype(F32) * ym + ga_ref[...].astype(F32) * ya
    mix = _dot(gated.astype(BF16), wo_ref[...])
    o_ref[...] = _layer_norm(ALPHA * x_ref[...] + mix, g_ref[...], b_ref[...])


def _mix(hg, oa, sg3, x, wbm, wba, wo, ln_g, ln_b, tb=512):
    t = x.shape[0]
    row = lambda i: (i, 0)
    const = lambda i: (0, 0)
    return pl.pallas_call(
        _mix_body,
        grid=(t // tb,),
        in_specs=[pl.BlockSpec((tb, M_WIDTH), row),
                  pl.BlockSpec((tb, A_WIDTH), row),
                  pl.BlockSpec((tb, D_MODEL), lambda i: (i, 1)),
                  pl.BlockSpec((tb, D_MODEL), lambda i: (i, 2)),
                  pl.BlockSpec((tb, D_MODEL), row),
                  pl.BlockSpec((M_WIDTH, D_MODEL), const),
                  pl.BlockSpec((A_WIDTH, D_MODEL), const),
                  pl.BlockSpec((D_MODEL, D_MODEL), const),
                  pl.BlockSpec((1, D_MODEL), const),
                  pl.BlockSpec((1, D_MODEL), const)],
        out_specs=pl.BlockSpec((tb, D_MODEL), row),
        out_shape=jax.ShapeDtypeStruct((t, D_MODEL), F32),
        compiler_params=_cparams(("parallel",)),
    )(hg, oa, sg3, sg3, x, wbm, wba, wo, ln_g, ln_b)


SUBLANES = 8
LOG2E = math.log2(math.e)


def _batcher_network(lo, hi):
    def merge(lo, hi, r):
        step = r * 2
        if step < hi - lo:
            yield from merge(lo, hi, step)
            yield from merge(lo + r, hi, step)
            yield from [(i, i + r) for i in range(lo + r, hi - r, step)]
        else:
            yield (lo, lo + r)
    if hi - lo >= 1:
        mid = lo + (hi - lo) // 2
        yield from _batcher_network(lo, mid)
        yield from _batcher_network(mid + 1, hi)
        yield from merge(lo, hi, 1)


_SORT16 = tuple(_batcher_network(0, TOPK - 1))
_BITONIC16 = tuple((i, i + s) for s in (8, 4, 2, 1) for i in range(TOPK) if (i // s) % 2 == 0)


def _compare_exchange(rows, net):
    rows = list(rows)
    for i, j in net:
        a, b = rows[i], rows[j]
        if b is None:
            continue
        if a is None:
            rows[i], rows[j] = b, None
        else:
            rows[i], rows[j] = jnp.maximum(a, b), jnp.minimum(a, b)
    return rows


def _top16_values(slabs):
    rows = list(slabs) + [None] * (TOPK - len(slabs))
    rows = _compare_exchange(rows, _SORT16)
    for shift in (4, 2, 1):
        merged = []
        for r in range(TOPK):
            a, b = rows[r], rows[TOPK - 1 - r]
            b = None if b is None else pltpu.roll(b, shift, axis=0)
            merged.append(b if a is None else a if b is None else jnp.maximum(a, b))
        rows = _compare_exchange(merged, _BITONIC16)
    return rows


def _by_sublane(rows, sub):
    out = rows[SUBLANES - 1]
    for g in range(SUBLANES - 2, -1, -1):
        out = jnp.where(sub == g, rows[g], out)
    return out


def _peer_sel_body(x_ref, wq_ref, sk_ref, s_ref, aux_ref):
    tb = x_ref.shape[0]
    qt = _dot_nt(wq_ref[...], x_ref[...].astype(BF16))
    sub = lax.broadcasted_iota(jnp.int32, (SUBLANES, tb), 0)
    for h in range(R_HEADS):
        top = []
        for p in range(2):
            r = h * 2 + p
            qhp = qt[r * HALF_KEY:(r + 1) * HALF_KEY, :].astype(BF16)
            st = _dot(sk_ref[r], qhp) * LOG2E
            s_ref[r] = st
            top.append(_top16_values([st[g * SUBLANES:(g + 1) * SUBLANES, :] for g in range(N_KEYS // SUBLANES)]))
        c1, c2 = top
        v1_lo, v1_hi = _by_sublane(c1[:SUBLANES], sub), _by_sublane(c1[SUBLANES:], sub)
        v2_hi = _by_sublane(c2[SUBLANES:], sub)
        firsts = [v1_lo] * SUBLANES + [c1[0], v1_hi]
        seconds = c2[:SUBLANES] + [v2_hi, c2[0]]
        cand = [f + s for f, s in zip(firsts, seconds)]
        cv = _top16_values(cand)
        tau, best = cv[TOPK - 1], cv[0]
        picked = [c >= tau for c in cand]
        z = sum(jnp.where(pk, jnp.exp2(c - best), 0.0) for pk, c in zip(picked, cand))
        shift = best[0:1, :] + jnp.log2(jnp.sum(z, axis=0, keepdims=True))
        shifted = [jnp.where(pk, (f - shift) + s, jnp.inf) for pk, f, s in zip(picked, firsts, seconds)]
        tau_shifted = functools.reduce(jnp.minimum, shifted)
        s_ref[2 * h] = s_ref[2 * h] - shift
        aux_ref[h:h + 1, :] = jnp.min(tau_shifted, axis=0, keepdims=True)


def _peer_sel(x, wq_t, subkeys, tb=256):
    t = x.shape[0]
    return pl.pallas_call(
        _peer_sel_body,
        grid=(t // tb,),
        in_specs=[pl.BlockSpec((tb, D_MODEL), lambda i: (i, 0)),
                  pl.BlockSpec((2 * R_HEADS * HALF_KEY, D_MODEL), lambda i: (0, 0)),
                  pl.BlockSpec((2 * R_HEADS, N_KEYS, HALF_KEY), lambda i: (0, 0, 0))],
        out_specs=[pl.BlockSpec((2 * R_HEADS, N_KEYS, tb), lambda i: (0, 0, i)),
                   pl.BlockSpec((R_HEADS, tb), lambda i: (0, i))],
        out_shape=[jax.ShapeDtypeStruct((2 * R_HEADS, N_KEYS, t), F32),
                   jax.ShapeDtypeStruct((R_HEADS, t), F32)],
        compiler_params=_cparams(("parallel",)),
    )(x, wq_t, subkeys)


def _gelu_tanh(x):
    return 0.5 * x * (1.0 + jnp.tanh(math.sqrt(2.0 / math.pi) * (x + 0.044715 * (x * x * x))))


def _peer_ffn_body(x_ref, u_ref, vt_ref, s_ref, aux_ref, pe_ref, plp_ref, plg_ref, g_ref, b_ref,
                   o_ref, xb_scr, acc_scr, p_scr, *, eb):
    e = pl.program_id(1)
    tb = x_ref.shape[0]

    @pl.when(e == 0)
    def _():
        xb_scr[...] = x_ref[...].astype(BF16)
        acc_scr[...] = jnp.zeros_like(acc_scr)

    act = _gelu_tanh(_dot_nt(u_ref[...], xb_scr[...]))
    rows_per_step = eb // N_KEYS
    assert rows_per_step == 8, "one aligned sublane group of first-half scores per step"
    i0 = pl.multiple_of(e * rows_per_step, rows_per_step)
    for ii in range(rows_per_step):
        for lg in range(tb // LANES):
            ls = slice(lg * LANES, (lg + 1) * LANES)
            w = jnp.zeros((N_KEYS, LANES), F32)
            for h in range(R_HEADS):
                s1 = s_ref[2 * h, pl.ds(i0, rows_per_step), ls][ii:ii + 1, :]
                s = s1 + s_ref[2 * h + 1, :, ls]
                w = w + jnp.where(s >= aux_ref[h:h + 1, ls], jnp.exp2(s), 0.0)
            p_scr[ii * N_KEYS:(ii + 1) * N_KEYS, ls] = (act[ii * N_KEYS:(ii + 1) * N_KEYS, ls] * w).astype(BF16)
    acc_scr[...] += _dot(vt_ref[...], p_scr[...])

    @pl.when(e == pl.num_programs(1) - 1)
    def _():
        x = x_ref[...]
        y = _layer_norm(ALPHA * x + acc_scr[...].T, g_ref[...], b_ref[...])
        ple = _dot(pe_ref[...].astype(BF16), plp_ref[...])
        gate = _sigmoid(_dot(y.astype(BF16), plg_ref[...]))
        o_ref[...] = y + ple * gate


def _peer_ffn(x, u, vt, scores, aux, pe, plp, plg, ln_g, ln_b, tb=512, eb=1024):
    t = x.shape[0]
    row = lambda i, e: (i, 0)
    const = lambda i, e: (0, 0)
    return pl.pallas_call(
        functools.partial(_peer_ffn_body, eb=eb),
        grid=(t // tb, N_EXPERTS // eb),
        in_specs=[pl.BlockSpec((tb, D_MODEL), row),
                  pl.BlockSpec((eb, D_MODEL), lambda i, e: (e, 0)),
                  pl.BlockSpec((D_MODEL, eb), lambda i, e: (0, e)),
                  pl.BlockSpec((2 * R_HEADS, N_KEYS, tb), lambda i, e: (0, 0, i)),
                  pl.BlockSpec((R_HEADS, tb), lambda i, e: (0, i)),
                  pl.BlockSpec((tb, PLE_DIM), row),
                  pl.BlockSpec((PLE_DIM, D_MODEL), const),
                  pl.BlockSpec((D_MODEL, D_MODEL), const),
                  pl.BlockSpec((1, D_MODEL), const),
                  pl.BlockSpec((1, D_MODEL), const)],
        out_specs=pl.BlockSpec((tb, D_MODEL), row),
        out_shape=jax.ShapeDtypeStruct((t, D_MODEL), F32),
        scratch_shapes=[pltpu.VMEM((tb, D_MODEL), BF16),
                        pltpu.VMEM((D_MODEL, tb), F32),
                        pltpu.VMEM((eb, tb), BF16)],
        compiler_params=_cparams(("parallel", "arbitrary")),
    )(x, u, vt, scores, aux, pe, plp, plg, ln_g, ln_b)


def _prep_layer(w_in, if_bias, norm_w, rel_bias, w_bm, w_ba, w_o, ln1g, ln1b, ln2g, ln2b,
                pq, psk, pu, pv, plp, plg):
    c = 0
    cols = {}
    for name, width in (("mq", M_WIDTH), ("mk", M_WIDTH), ("mv", M_WIDTH), ("mi", M_HEADS), ("mf", M_HEADS),
                        ("mo", M_WIDTH), ("aq", A_WIDTH), ("ak", A_WIDTH), ("av", A_WIDTH),
                        ("gm", D_MODEL), ("ga", D_MODEL)):
        cols[name] = w_in[:, c:c + width]
        c += width
    cat = lambda names: jnp.concatenate([cols[n] for n in names], axis=1)
    ones = lambda n: jnp.ones((1, n), F32)
    w_if = jnp.pad(cat(("mi", "mf")), ((0, 0), (0, LANES - 2 * M_HEADS)))
    b_if = jnp.pad(if_bias.reshape(1, 2 * M_HEADS).astype(F32), ((0, 0), (0, LANES - 2 * M_HEADS)))
    return dict(
        w_mqkv=cat(("mq", "mk", "mv")).astype(BF16),
        s_mqkv=jnp.concatenate([ones(M_WIDTH), ones(M_WIDTH) * (M_HDIM ** -0.5), ones(M_WIDTH)], axis=1),
        w_sig=cat(("mo", "gm", "ga")).astype(BF16), s_sig=ones(3 * D_MODEL),
        w_aqkv=cat(("aq", "ak", "av")).astype(BF16), s_aqkv=ones(3 * A_WIDTH),
        w_if=w_if, b_if=b_if,
        norm_w=norm_w.reshape(1, M_WIDTH).astype(F32),
        bias=_attn_bias(rel_bias),
        w_bm=w_bm.astype(BF16), w_ba=w_ba.astype(BF16), w_o=w_o.astype(BF16),
        ln1g=ln1g.reshape(1, D_MODEL), ln1b=ln1b.reshape(1, D_MODEL),
        ln2g=ln2g.reshape(1, D_MODEL), ln2b=ln2b.reshape(1, D_MODEL),
        wq_t=pq.T.astype(BF16),
        subkeys=psk.reshape(2 * R_HEADS, N_KEYS, HALF_KEY).astype(BF16),
        u=pu.astype(BF16), vt=pv.T.astype(BF16),
        plp=plp.astype(BF16), plg=plg.astype(BF16),
    )


def _layer(x3, pe3, lw, state, cache):
    b, s, _ = x3.shape
    t = b * s
    x = x3.reshape(t, D_MODEL)
    mqkv, = _proj(x, lw["w_mqkv"], lw["s_mqkv"], None, (BF16,), tb=1024, nb=1024)
    sig3, = _proj(x, lw["w_sig"], lw["s_sig"], "sigmoid", (BF16,), tb=1024, nb=1024)
    aqkv, akv32 = _proj(x, lw["w_aqkv"], lw["s_aqkv"], None, (BF16, F32), tb=1024, nb=512)
    gates = _gates(x, lw["w_if"], lw["b_if"])

    if state is None:
        c0 = jnp.zeros((b, M_HEADS, M_HDIM, M_HDIM), F32)
        n0 = jnp.zeros((b, 1, M_WIDTH), F32)
        m0 = jnp.zeros((b, 1, LANES), F32)
    else:
        c0 = state[0].astype(F32)
        n0 = state[1].astype(F32).reshape(b, 1, M_WIDTH)
        m0 = jnp.pad(state[2].astype(F32), ((0, 0), (0, LANES - M_HEADS))).reshape(b, 1, LANES)
    hg, c_new, n_new, m_new = _mlstm(mqkv.reshape(b, s, 3 * M_WIDTH), gates.reshape(b, s, LANES),
                                     sig3.reshape(b, s, 3 * D_MODEL), lw["norm_w"], c0, n0, m0)
    n_new = n_new.reshape(b, M_HEADS, M_HDIM)
    m_new = m_new.reshape(b, LANES)[:, :M_HEADS]

    a3 = aqkv.reshape(b, s, 3 * A_WIDTH)
    q, k, v = a3[..., :A_WIDTH], a3[..., A_WIDTH:2 * A_WIDTH], a3[..., 2 * A_WIDTH:]
    kv32 = akv32.reshape(b, s, 3 * A_WIDTH)
    if cache is None:
        hist_k = jnp.zeros((b, ATT_REACH, A_WIDTH), BF16)
        hist_v = hist_k
        first_valid_block = ATT_REACH // ATT_Q
        keep = min(ATT_REACH, s)
        k_out = kv32[:, s - keep:, A_WIDTH:2 * A_WIDTH].reshape(b, keep, A_HEADS, A_HDIM)
        v_out = kv32[:, s - keep:, 2 * A_WIDTH:].reshape(b, keep, A_HEADS, A_HDIM)
    else:
        hist_k = cache[0].reshape(b, ATT_REACH, A_WIDTH).astype(BF16)
        hist_v = cache[1].reshape(b, ATT_REACH, A_WIDTH).astype(BF16)
        first_valid_block = 0
        k_out = kv32[..., A_WIDTH:2 * A_WIDTH].reshape(b, s, A_HEADS, A_HDIM)
        v_out = kv32[..., 2 * A_WIDTH:].reshape(b, s, A_HEADS, A_HDIM)
    sq = -(-s // ATT_Q) * ATT_Q
    tail = ((0, 0), (0, sq - s), (0, 0))
    oa = _attn(jnp.pad(q, tail), jnp.pad(jnp.concatenate([hist_k, k], axis=1), tail),
               jnp.pad(jnp.concatenate([hist_v, v], axis=1), tail), lw["bias"], first_valid_block)
    oa = oa[:, :s].reshape(t, A_WIDTH)

    x1 = _mix(hg.reshape(t, M_WIDTH), oa, sig3, x, lw["w_bm"], lw["w_ba"], lw["w_o"], lw["ln1g"], lw["ln1b"])
    scores, aux = _peer_sel(x1, lw["wq_t"], lw["subkeys"])
    x3_out = _peer_ffn(x1, lw["u"], lw["vt"], scores, aux, pe3.reshape(t, PLE_DIM), lw["plp"], lw["plg"],
                       lw["ln2g"], lw["ln2b"])
    return x3_out.reshape(b, s, D_MODEL), (c_new, n_new, m_new), (k_out, v_out)


def kernel(x_prompt, x_sample, cache_attn_k, cache_attn_v, state_mlstm_C, state_mlstm_n, state_mlstm_m,
           p_prompt, p_sample, w_in, mlstm_if_bias, mlstm_norm_w, attn_rel_bias, w_branch_m, w_branch_a,
           w_out, ln1_g, ln1_b, ln2_g, ln2_b, peer_wq, peer_subkeys, peer_u, peer_v, ple_proj, ple_gate):
    layer_w = (w_in, mlstm_if_bias, mlstm_norm_w, attn_rel_bias, w_branch_m, w_branch_a, w_out,
               ln1_g, ln1_b, ln2_g, ln2_b, peer_wq, peer_subkeys, peer_u, peer_v, ple_proj, ple_gate)
    yp, ys = x_prompt, x_sample
    outs_p = [[] for _ in range(5)]
    outs_s = [[] for _ in range(5)]
    for i in range(DEPTH):
        lw = _prep_layer(*(w[i] for w in layer_w))
        yp, (cp, np_, mp), (kp, vp) = _layer(yp, p_prompt[i], lw, None, None)
        ys, (cs, ns, ms), (ks, vs) = _layer(
            ys, p_sample[i], lw, (state_mlstm_C[i], state_mlstm_n[i], state_mlstm_m[i]),
            (cache_attn_k[i], cache_attn_v[i]))
        for lst, val in zip(outs_p, (kp, vp, cp, np_, mp)):
            lst.append(val)
        for lst, val in zip(outs_s, (ks, vs, cs, ns, ms)):
            lst.append(val)
    st = lambda l: jnp.stack(l, axis=0)
    return (yp, ys) + tuple(st(l) for l in outs_p) + tuple(st(l) for l in outs_s)
```

```python
import functools
import math

import jax
import jax.numpy as jnp
from jax import lax
from jax.experimental import pallas as pl
from jax.experimental.pallas import tpu as pltpu

F32 = jnp.float32
BF16 = jnp.bfloat16

D_MODEL = 1024
CHUNK = 64
M_HEADS = 4
M_HDIM = 256
M_WIDTH = 1024
A_HEADS = 8
A_HDIM = 64
A_WIDTH = 512
PAST_CHUNKS = 8
ATT_REACH = 512
REL_CLIP = 128
R_HEADS = 8
N_KEYS = 128
N_EXPERTS = N_KEYS * N_KEYS
HALF_KEY = 128
TOPK = 16
PLE_DIM = 256
DEPTH = 2
ALPHA = (2 * DEPTH) ** 0.25
LN_EPS = 1e-5

LANES = 128
MLSTM_ROWS = 2
ATT_Q = 2 * CHUNK
ATT_WIN = ATT_REACH + ATT_Q
VMEM_LIMIT = 48 * 1024 * 1024
NEG_INF = float("-inf")


def _cparams(sem):
    return pltpu.CompilerParams(dimension_semantics=sem, vmem_limit_bytes=VMEM_LIMIT)


def _sigmoid(x):
    return 1.0 / (1.0 + jnp.exp(-x))


def _dot(a, b):
    return jnp.dot(a, b, preferred_element_type=F32)


def _dot_nt(a, b):
    return lax.dot_general(a, b, (((1,), (1,)), ((), ())), preferred_element_type=F32)


def _dot_tn(a, b):
    return lax.dot_general(a, b, (((0,), (0,)), ((), ())), preferred_element_type=F32)


def _layer_norm(y, g, b):
    mu = jnp.mean(y, axis=-1, keepdims=True)
    d = y - mu
    var = jnp.mean(d * d, axis=-1, keepdims=True)
    return d * lax.rsqrt(var + LN_EPS) * g + b


def _proj_body(x_ref, w_ref, s_ref, *o_refs, act):
    acc = _dot(x_ref[...].astype(BF16), w_ref[...]) * s_ref[...]
    if act == "sigmoid":
        acc = _sigmoid(acc)
    for o in o_refs:
        o[...] = acc.astype(o.dtype)


def _proj(x, w, scale, act, out_dtypes, tb=512, nb=512):
    t, k = x.shape
    n = w.shape[1]
    return pl.pallas_call(
        functools.partial(_proj_body, act=act),
        grid=(t // tb, n // nb),
        in_specs=[pl.BlockSpec((tb, k), lambda i, j: (i, 0)),
                  pl.BlockSpec((k, nb), lambda i, j: (0, j)),
                  pl.BlockSpec((1, nb), lambda i, j: (0, j))],
        out_specs=[pl.BlockSpec((tb, nb), lambda i, j: (i, j)) for _ in out_dtypes],
        out_shape=[jax.ShapeDtypeStruct((t, n), d) for d in out_dtypes],
        compiler_params=_cparams(("parallel", "arbitrary")),
    )(x, w, scale)


def _split3(v):
    hi = v.astype(BF16)
    r = v - hi.astype(F32)
    mid = r.astype(BF16)
    lo = (r - mid.astype(F32)).astype(BF16)
    return hi, mid, lo


def _gate_body(x_ref, w_ref, b_ref, o_ref):
    xh, xm, xl = _split3(x_ref[...])
    wh, wm, wl = _split3(w_ref[...])
    z = (_dot(xh, wh) + (_dot(xh, wm) + _dot(xm, wh))
         + (_dot(xh, wl) + _dot(xm, wm) + _dot(xl, wh))) + b_ref[...]
    col = lax.broadcasted_iota(jnp.int32, z.shape, 1)
    log_sig = jnp.minimum(z, 0.0) - jnp.log1p(jnp.exp(-jnp.abs(z)))
    o_ref[...] = jnp.where(col < M_HEADS, z, log_sig)


def _gates(x, w_if, b_if, tb=512):
    t, k = x.shape
    return pl.pallas_call(
        _gate_body,
        grid=(t // tb,),
        in_specs=[pl.BlockSpec((tb, k), lambda i: (i, 0)),
                  pl.BlockSpec((k, LANES), lambda i: (0, 0)),
                  pl.BlockSpec((1, LANES), lambda i: (0, 0))],
        out_specs=pl.BlockSpec((tb, LANES), lambda i: (i, 0)),
        out_shape=jax.ShapeDtypeStruct((t, LANES), F32),
        compiler_params=_cparams(("parallel",)),
    )(x, w_if, b_if)


def _mlstm_body(qkv_ref, g_ref, og_ref, nw_ref, c0_ref, n0_ref, m0_ref,
                h_ref, c_ref, n_ref, m_ref):
    @pl.when(pl.program_id(1) == 0)
    def _():
        c_ref[...] = c0_ref[...]
        n_ref[...] = n0_ref[...]
        m_ref[...] = m0_ref[...]

    for bb in range(qkv_ref.shape[0]):
        _mlstm_chunk_update(bb, qkv_ref, g_ref, og_ref, nw_ref, h_ref, c_ref, n_ref, m_ref)


def _mlstm_chunk_update(bb, qkv_ref, g_ref, og_ref, nw_ref, h_ref, c_ref, n_ref, m_ref):
    L = CHUNK
    g = g_ref[bb]
    gt = jnp.concatenate([g, jnp.zeros((LANES - L, LANES), F32)], axis=0).T
    t_i = lax.broadcasted_iota(jnp.int32, (L, L), 0)
    s_i = lax.broadcasted_iota(jnp.int32, (L, L), 1)
    causal = s_i <= t_i
    lane = lax.broadcasted_iota(jnp.int32, (1, LANES), 1)
    m_all = m_ref[bb]
    m_next = m_all
    for h in range(M_HEADS):
        sl = slice(h * M_HDIM, (h + 1) * M_HDIM)
        q = qkv_ref[bb, :, h * M_HDIM:(h + 1) * M_HDIM]
        k = qkv_ref[bb, :, M_WIDTH + h * M_HDIM:M_WIDTH + (h + 1) * M_HDIM]
        v = qkv_ref[bb, :, 2 * M_WIDTH + h * M_HDIM:2 * M_WIDTH + (h + 1) * M_HDIM]
        ig_col = g[:, h:h + 1]
        lf_col = g[:, M_HEADS + h:M_HEADS + h + 1]
        ig_row = gt[h:h + 1, :L]
        lf_row = gt[M_HEADS + h:M_HEADS + h + 1, :L]
        b_col = jnp.sum(jnp.where(causal, lf_row, 0.0), axis=1, keepdims=True)
        b_row = jnp.sum(jnp.where(t_i <= s_i, lf_col, 0.0), axis=0, keepdims=True)
        m_prev = jnp.sum(jnp.where(lane == h, m_all, 0.0), axis=1, keepdims=True)
        inter = b_col + m_prev
        dmat = jnp.where(causal, b_col - b_row + ig_row, NEG_INF)
        m_t = jnp.maximum(inter, jnp.max(dmat, axis=1, keepdims=True))
        w_inter = jnp.exp(inter - m_t)
        w_intra = jnp.exp(dmat - m_t)
        a = w_intra * _dot_nt(q, k)
        c_h = c_ref[bb, h]
        n_h = n_ref[bb, :, sl]
        qf = q.astype(F32)
        num = w_inter * _dot_nt(q, c_h.astype(BF16)) + _dot(a.astype(BF16), v)
        den = w_inter * jnp.sum(qf * n_h, axis=1, keepdims=True) + jnp.sum(a, axis=1, keepdims=True)
        hh = num * (1.0 / jnp.maximum(jnp.abs(den), jnp.exp(-m_t)))
        mu = jnp.mean(hh, axis=1, keepdims=True)
        dv = hh - mu
        var = jnp.mean(dv * dv, axis=1, keepdims=True)
        hn = dv * lax.rsqrt(var + LN_EPS) * nw_ref[:, sl]
        h_ref[bb, :, sl] = (og_ref[bb, :, sl].astype(F32) * hn).astype(h_ref.dtype)
        m_new = m_t[L - 1:L, :]
        b_last = b_col[L - 1:L, :]
        g_state = jnp.exp(b_last + m_prev - m_new)
        g_s = jnp.exp(b_last - b_col + ig_col - m_new)
        vg = (g_s * v.astype(F32)).astype(BF16)
        c_ref[bb, h] = g_state * c_h + _dot_tn(vg, k)
        n_ref[bb, :, sl] = g_state * n_h + jnp.sum(g_s * k.astype(F32), axis=0, keepdims=True)
        m_next = jnp.where(lane == h, m_new, m_next)
    m_ref[bb] = m_next


def _mlstm(qkv, gates, og3, norm_w, c0, n0, m0):
    b, s, _ = qkv.shape
    nc = s // CHUNK
    nb = MLSTM_ROWS
    return pl.pallas_call(
        _mlstm_body,
        grid=(b // nb, nc),
        in_specs=[pl.BlockSpec((nb, CHUNK, 3 * M_WIDTH), lambda i, c: (i, c, 0)),
                  pl.BlockSpec((nb, CHUNK, LANES), lambda i, c: (i, c, 0)),
                  pl.BlockSpec((nb, CHUNK, M_WIDTH), lambda i, c: (i, c, 0)),
                  pl.BlockSpec((1, M_WIDTH), lambda i, c: (0, 0)),
                  pl.BlockSpec((nb, M_HEADS, M_HDIM, M_HDIM), lambda i, c: (i, 0, 0, 0)),
                  pl.BlockSpec((nb, 1, M_WIDTH), lambda i, c: (i, 0, 0)),
                  pl.BlockSpec((nb, 1, LANES), lambda i, c: (i, 0, 0))],
        out_specs=[pl.BlockSpec((nb, CHUNK, M_WIDTH), lambda i, c: (i, c, 0)),
                   pl.BlockSpec((nb, M_HEADS, M_HDIM, M_HDIM), lambda i, c: (i, 0, 0, 0)),
                   pl.BlockSpec((nb, 1, M_WIDTH), lambda i, c: (i, 0, 0)),
                   pl.BlockSpec((nb, 1, LANES), lambda i, c: (i, 0, 0))],
        out_shape=[jax.ShapeDtypeStruct((b, s, M_WIDTH), BF16),
                   jax.ShapeDtypeStruct((b, M_HEADS, M_HDIM, M_HDIM), F32),
                   jax.ShapeDtypeStruct((b, 1, M_WIDTH), F32),
                   jax.ShapeDtypeStruct((b, 1, LANES), F32)],
        compiler_params=_cparams(("parallel", "arbitrary")),
    )(qkv, gates, og3, norm_w, c0, n0, m0)


def _attn_body(q_ref, k_ref, v_ref, bias_ref, o_ref, *, first_valid_block):
    p = pl.program_id(1)
    row0 = pl.multiple_of(p * ATT_Q, ATT_Q)
    kw = k_ref[0, pl.ds(row0, ATT_WIN), :]
    vw = v_ref[0, pl.ds(row0, ATT_WIN), :]
    key_j = lax.broadcasted_iota(jnp.int32, (ATT_Q, ATT_WIN), 1)
    key_ok = key_j >= (first_valid_block - p) * ATT_Q
    lane = lax.broadcasted_iota(jnp.int32, (1, LANES), 1)
    for pair in range(A_HEADS // 2):
        cs = slice(pair * LANES, (pair + 1) * LANES)
        qp = q_ref[0, :, cs]
        kp = kw[:, cs]
        vp = vw[:, cs]
        o_pair = jnp.zeros((ATT_Q, LANES), F32)
        for e in range(2):
            head = 2 * pair + e
            in_head = (lane >= e * A_HDIM) & (lane < (e + 1) * A_HDIM)
            qm = jnp.where(in_head, qp, jnp.zeros_like(qp))
            s = _dot_nt(qm, kp) * (A_HDIM ** -0.5) + bias_ref[head]
            s = jnp.where(key_ok, s, NEG_INF)
            m = jnp.max(s, axis=1, keepdims=True)
            pe = jnp.exp(s - m)
            l = jnp.sum(pe, axis=1, keepdims=True)
            vm = jnp.where(in_head, vp, jnp.zeros_like(vp))
            o_pair = o_pair + _dot(pe.astype(BF16), vm) * (1.0 / l)
        o_ref[0, :, cs] = o_pair.astype(o_ref.dtype)


def _attn(q, k_pad, v_pad, bias, first_valid_block):
    b, sq, _ = q.shape
    sk = k_pad.shape[1]
    return pl.pallas_call(
        functools.partial(_attn_body, first_valid_block=first_valid_block),
        grid=(b, sq // ATT_Q),
        in_specs=[pl.BlockSpec((1, ATT_Q, A_WIDTH), lambda i, p: (i, p, 0)),
                  pl.BlockSpec((1, sk, A_WIDTH), lambda i, p: (i, 0, 0)),
                  pl.BlockSpec((1, sk, A_WIDTH), lambda i, p: (i, 0, 0)),
                  pl.BlockSpec((A_HEADS, ATT_Q, ATT_WIN), lambda i, p: (0, 0, 0))],
        out_specs=pl.BlockSpec((1, ATT_Q, A_WIDTH), lambda i, p: (i, p, 0)),
        out_shape=jax.ShapeDtypeStruct((b, sq, A_WIDTH), BF16),
        compiler_params=_cparams(("parallel", "arbitrary")),
    )(q, k_pad, v_pad, bias)


def _attn_bias(rel_bias):
    i = jnp.arange(ATT_Q)[:, None]
    j = jnp.arange(ATT_WIN)[None, :]
    n_diag = ATT_Q + ATT_WIN
    k = jnp.arange(n_diag)
    e = rel_bias.astype(F32)[:, jnp.clip(ATT_REACH + ATT_Q - 1 - k, -REL_CLIP, REL_CLIP) + REL_CLIP]
    skew = jnp.tile(e, (1, ATT_Q))[:, :ATT_Q * (n_diag - 1)].reshape(A_HEADS, ATT_Q, n_diag - 1)
    bias = skew[:, :, ATT_Q - 1:ATT_Q - 1 + ATT_WIN]
    qc = i // CHUNK
    kc = j // CHUNK - PAST_CHUNKS
    visible = (kc <= qc) & (kc >= qc - PAST_CHUNKS)
    return jnp.where(visible[None], bias, NEG_INF)


def _mix_body(hg_ref, oa_ref, gm_ref, ga_ref, x_ref, wbm_ref, wba_ref, wo_ref, g_ref, b_ref, o_ref):
    ym = _dot(hg_ref[...], wbm_ref[...])
    ya = _dot(oa_ref[...], wba_ref[...])
    gated = gm_ref[...].astype(F32) * ym + ga_ref[...].astype(F32) * ya
    mix = _dot(gated.astype(BF16), wo_ref[...])
    o_ref[...] = _layer_norm(ALPHA * x_ref[...] + mix, g_ref[...], b_ref[...])


def _mix(hg, oa, sg3, x, wbm, wba, wo, ln_g, ln_b, tb=512):
    t = x.shape[0]
    row = lambda i: (i, 0)
    const = lambda i: (0, 0)
    return pl.pallas_call(
        _mix_body,
        grid=(t // tb,),
        in_specs=[pl.BlockSpec((tb, M_WIDTH), row),
                  pl.BlockSpec((tb, A_WIDTH), row),
                  pl.BlockSpec((tb, D_MODEL), lambda i: (i, 1)),
                  pl.BlockSpec((tb, D_MODEL), lambda i: (i, 2)),
                  pl.BlockSpec((tb, D_MODEL), row),
                  pl.BlockSpec((M_WIDTH, D_MODEL), const),
                  pl.BlockSpec((A_WIDTH, D_MODEL), const),
                  pl.BlockSpec((D_MODEL, D_MODEL), const),
                  pl.BlockSpec((1, D_MODEL), const),
                  pl.BlockSpec((1, D_MODEL), const)],
        out_specs=pl.BlockSpec((tb, D_MODEL), row),
        out_shape=jax.ShapeDtypeStruct((t, D_MODEL), F32),
        compiler_params=_cparams(("parallel",)),
    )(hg, oa, sg3, sg3, x, wbm, wba, wo, ln_g, ln_b)


SUBLANES = 8
LOG2E = math.log2(math.e)


def _batcher_network(lo, hi):
    def merge(lo, hi, r):
        step = r * 2
        if step < hi - lo:
            yield from merge(lo, hi, step)
            yield from merge(lo + r, hi, step)
            yield from [(i, i + r) for i in range(lo + r, hi - r, step)]
        else:
            yield (lo, lo + r)
    if hi - lo >= 1:
        mid = lo + (hi - lo) // 2
        yield from _batcher_network(lo, mid)
        yield from _batcher_network(mid + 1, hi)
        yield from merge(lo, hi, 1)


_SORT16 = tuple(_batcher_network(0, TOPK - 1))
_BITONIC16 = tuple((i, i + s) for s in (8, 4, 2, 1) for i in range(TOPK) if (i // s) % 2 == 0)


def _compare_exchange(rows, net):
    rows = list(rows)
    for i, j in net:
        a, b = rows[i], rows[j]
        if b is None:
            continue
        if a is None:
            rows[i], rows[j] = b, None
        else:
            rows[i], rows[j] = jnp.maximum(a, b), jnp.minimum(a, b)
    return rows


def _top16_values(slabs):
    rows = list(slabs) + [None] * (TOPK - len(slabs))
    rows = _compare_exchange(rows, _SORT16)
    for shift in (4, 2, 1):
        merged = []
        for r in range(TOPK):
            a, b = rows[r], rows[TOPK - 1 - r]
            b = None if b is None else pltpu.roll(b, shift, axis=0)
            merged.append(b if a is None else a if b is None else jnp.maximum(a, b))
        rows = _compare_exchange(merged, _BITONIC16)
    return rows


def _by_sublane(rows, sub):
    out = rows[SUBLANES - 1]
    for g in range(SUBLANES - 2, -1, -1):
        out = jnp.where(sub == g, rows[g], out)
    return out


def _peer_sel_body(x_ref, wq_ref, sk_ref, s_ref, aux_ref):
    tb = x_ref.shape[0]
    qt = _dot_nt(wq_ref[...], x_ref[...].astype(BF16))
    sub = lax.broadcasted_iota(jnp.int32, (SUBLANES, tb), 0)
    for h in range(R_HEADS):
        top = []
        for p in range(2):
            r = h * 2 + p
            qhp = qt[r * HALF_KEY:(r + 1) * HALF_KEY, :].astype(BF16)
            st = _dot(sk_ref[r], qhp) * LOG2E
            s_ref[r] = st
            top.append(_top16_values([st[g * SUBLANES:(g + 1) * SUBLANES, :] for g in range(N_KEYS // SUBLANES)]))
        c1, c2 = top
        v1_lo, v1_hi = _by_sublane(c1[:SUBLANES], sub), _by_sublane(c1[SUBLANES:], sub)
        v2_hi = _by_sublane(c2[SUBLANES:], sub)
        firsts = [v1_lo] * SUBLANES + [c1[0], v1_hi]
        seconds = c2[:SUBLANES] + [v2_hi, c2[0]]
        cand = [f + s for f, s in zip(firsts, seconds)]
        cv = _top16_values(cand)
        tau, best = cv[TOPK - 1], cv[0]
        picked = [c >= tau for c in cand]
        z = sum(jnp.where(pk, jnp.exp2(c - best), 0.0) for pk, c in zip(picked, cand))
        shift = best[0:1, :] + jnp.log2(jnp.sum(z, axis=0, keepdims=True))
        shifted = [jnp.where(pk, (f - shift) + s, jnp.inf) for pk, f, s in zip(picked, firsts, seconds)]
        tau_shifted = functools.reduce(jnp.minimum, shifted)
        s_ref[2 * h] = s_ref[2 * h] - shift
        aux_ref[h:h + 1, :] = jnp.min(tau_shifted, axis=0, keepdims=True)


def _peer_sel(x, wq_t, subkeys, tb=256):
    t = x.shape[0]
    return pl.pallas_call(
        _peer_sel_body,
        grid=(t // tb,),
        in_specs=[pl.BlockSpec((tb, D_MODEL), lambda i: (i, 0)),
                  pl.BlockSpec((2 * R_HEADS * HALF_KEY, D_MODEL), lambda i: (0, 0)),
                  pl.BlockSpec((2 * R_HEADS, N_KEYS, HALF_KEY), lambda i: (0, 0, 0))],
        out_specs=[pl.BlockSpec((2 * R_HEADS, N_KEYS, tb), lambda i: (0, 0, i)),
                   pl.BlockSpec((R_HEADS, tb), lambda i: (0, i))],
        out_shape=[jax.ShapeDtypeStruct((2 * R_HEADS, N_KEYS, t), F32),
                   jax.ShapeDtypeStruct((R_HEADS, t), F32)],
        compiler_params=_cparams(("parallel",)),
    )(x, wq_t, subkeys)


def _gelu_tanh(x):
    return 0.5 * x * (1.0 + jnp.tanh(math.sqrt(2.0 / math.pi) * (x + 0.044715 * (x * x * x))))


PEER_TILE = SUBLANES * N_KEYS
PEER_TILES = N_EXPERTS // PEER_TILE


def _peer_ffn_body(x_ref, u_ref, vt_ref, s_ref, aux_ref, pe_ref, plp_ref, plg_ref, g_ref, b_ref,
                   o_ref, xb_scr, acc_scr, p_scr):
    e = pl.program_id(1)
    tb = x_ref.shape[0]

    @pl.when(e == 0)
    def _():
        xb_scr[...] = x_ref[...].astype(BF16)
        acc_scr[...] = jnp.zeros_like(acc_scr)

    act = _gelu_tanh(_dot_nt(u_ref[...], xb_scr[...]).astype(BF16))
    i0 = pl.multiple_of(e * SUBLANES, SUBLANES)
    for ii in range(SUBLANES):
        rs = slice(ii * N_KEYS, (ii + 1) * N_KEYS)
        for lg in range(tb // LANES):
            ls = slice(lg * LANES, (lg + 1) * LANES)
            w = jnp.zeros((N_KEYS, LANES), F32)
            for h in range(R_HEADS):
                s1 = s_ref[2 * h, pl.ds(i0, SUBLANES), ls][ii:ii + 1, :]
                s = s1 + s_ref[2 * h + 1, :, ls]
                w = w + jnp.where(s >= aux_ref[h:h + 1, ls], jnp.exp2(s), 0.0)
            p_scr[rs, ls] = act[rs, ls] * w.astype(BF16)
    acc_scr[...] += _dot(vt_ref[0], p_scr[...])

    @pl.when(e == pl.num_programs(1) - 1)
    def _():
        x = x_ref[...]
        y = _layer_norm(ALPHA * x + acc_scr[...].T, g_ref[...], b_ref[...])
        ple = _dot(pe_ref[...].astype(BF16), plp_ref[...])
        gate = _sigmoid(_dot(y.astype(BF16), plg_ref[...]))
        o_ref[...] = y + ple * gate


def _peer_ffn(x, u, vt_tiles, scores, aux, pe, plp, plg, ln_g, ln_b, tb=512):
    t = x.shape[0]
    row = lambda i, e: (i, 0)
    const = lambda i, e: (0, 0)
    return pl.pallas_call(
        _peer_ffn_body,
        grid=(t // tb, PEER_TILES),
        in_specs=[pl.BlockSpec((tb, D_MODEL), row),
                  pl.BlockSpec((PEER_TILE, D_MODEL), lambda i, e: (e, 0)),
                  pl.BlockSpec((1, D_MODEL, PEER_TILE), lambda i, e: (e, 0, 0)),
                  pl.BlockSpec((2 * R_HEADS, N_KEYS, tb), lambda i, e: (0, 0, i)),
                  pl.BlockSpec((R_HEADS, tb), lambda i, e: (0, i)),
                  pl.BlockSpec((tb, PLE_DIM), row),
                  pl.BlockSpec((PLE_DIM, D_MODEL), const),
                  pl.BlockSpec((D_MODEL, D_MODEL), const),
                  pl.BlockSpec((1, D_MODEL), const),
                  pl.BlockSpec((1, D_MODEL), const)],
        out_specs=pl.BlockSpec((tb, D_MODEL), row),
        out_shape=jax.ShapeDtypeStruct((t, D_MODEL), F32),
        scratch_shapes=[pltpu.VMEM((tb, D_MODEL), BF16),
                        pltpu.VMEM((D_MODEL, tb), F32),
                        pltpu.VMEM((PEER_TILE, tb), BF16)],
        compiler_params=_cparams(("parallel", "arbitrary")),
    )(x, u, vt_tiles, scores, aux, pe, plp, plg, ln_g, ln_b)


def _prep_layer(w_in, if_bias, norm_w, rel_bias, w_bm, w_ba, w_o, ln1g, ln1b, ln2g, ln2b,
                pq, psk, pu, pv, plp, plg):
    c = 0
    cols = {}
    for name, width in (("mq", M_WIDTH), ("mk", M_WIDTH), ("mv", M_WIDTH), ("mi", M_HEADS), ("mf", M_HEADS),
                        ("mo", M_WIDTH), ("aq", A_WIDTH), ("ak", A_WIDTH), ("av", A_WIDTH),
                        ("gm", D_MODEL), ("ga", D_MODEL)):
        cols[name] = w_in[:, c:c + width]
        c += width
    cat = lambda names: jnp.concatenate([cols[n] for n in names], axis=1)
    ones = lambda n: jnp.ones((1, n), F32)
    w_if = jnp.pad(cat(("mi", "mf")), ((0, 0), (0, LANES - 2 * M_HEADS)))
    b_if = jnp.pad(if_bias.reshape(1, 2 * M_HEADS).astype(F32), ((0, 0), (0, LANES - 2 * M_HEADS)))
    return dict(
        w_mqkv=cat(("mq", "mk", "mv")).astype(BF16),
        s_mqkv=jnp.concatenate([ones(M_WIDTH), ones(M_WIDTH) * (M_HDIM ** -0.5), ones(M_WIDTH)], axis=1),
        w_sig=cat(("mo", "gm", "ga")).astype(BF16), s_sig=ones(3 * D_MODEL),
        w_aqkv=cat(("aq", "ak", "av")).astype(BF16), s_aqkv=ones(3 * A_WIDTH),
        w_if=w_if, b_if=b_if,
        norm_w=norm_w.reshape(1, M_WIDTH).astype(F32),
        bias=_attn_bias(rel_bias),
        w_bm=w_bm.astype(BF16), w_ba=w_ba.astype(BF16), w_o=w_o.astype(BF16),
        ln1g=ln1g.reshape(1, D_MODEL), ln1b=ln1b.reshape(1, D_MODEL),
        ln2g=ln2g.reshape(1, D_MODEL), ln2b=ln2b.reshape(1, D_MODEL),
        wq_t=pq.T.astype(BF16),
        subkeys=psk.reshape(2 * R_HEADS, N_KEYS, HALF_KEY).astype(BF16),
        u=pu.astype(BF16),
        vt=pv.reshape(PEER_TILES, PEER_TILE, D_MODEL).transpose(0, 2, 1).astype(BF16),
        plp=plp.astype(BF16), plg=plg.astype(BF16),
    )


def _layer(x3, pe3, lw, state, cache):
    b, s, _ = x3.shape
    t = b * s
    x = x3.reshape(t, D_MODEL)
    mqkv, = _proj(x, lw["w_mqkv"], lw["s_mqkv"], None, (BF16,), tb=1024, nb=1024)
    sig3, = _proj(x, lw["w_sig"], lw["s_sig"], "sigmoid", (BF16,), tb=1024, nb=1024)
    aqkv, akv32 = _proj(x, lw["w_aqkv"], lw["s_aqkv"], None, (BF16, F32), tb=1024, nb=512)
    gates = _gates(x, lw["w_if"], lw["b_if"])

    if state is None:
        c0 = jnp.zeros((b, M_HEADS, M_HDIM, M_HDIM), F32)
        n0 = jnp.zeros((b, 1, M_WIDTH), F32)
        m0 = jnp.zeros((b, 1, LANES), F32)
    else:
        c0 = state[0].astype(F32)
        n0 = state[1].astype(F32).reshape(b, 1, M_WIDTH)
        m0 = jnp.pad(state[2].astype(F32), ((0, 0), (0, LANES - M_HEADS))).reshape(b, 1, LANES)
    hg, c_new, n_new, m_new = _mlstm(mqkv.reshape(b, s, 3 * M_WIDTH), gates.reshape(b, s, LANES),
                                     sig3.reshape(b, s, 3 * D_MODEL), lw["norm_w"], c0, n0, m0)
    n_new = n_new.reshape(b, M_HEADS, M_HDIM)
    m_new = m_new.reshape(b, LANES)[:, :M_HEADS]

    a3 = aqkv.reshape(b, s, 3 * A_WIDTH)
    q, k, v = a3[..., :A_WIDTH], a3[..., A_WIDTH:2 * A_WIDTH], a3[..., 2 * A_WIDTH:]
    kv32 = akv32.reshape(b, s, 3 * A_WIDTH)
    if cache is None:
        hist_k = jnp.zeros((b, ATT_REACH, A_WIDTH), BF16)
        hist_v = hist_k
        first_valid_block = ATT_REACH // ATT_Q
        keep = min(ATT_REACH, s)
        k_out = kv32[:, s - keep:, A_WIDTH:2 * A_WIDTH].reshape(b, keep, A_HEADS, A_HDIM)
        v_out = kv32[:, s - keep:, 2 * A_WIDTH:].reshape(b, keep, A_HEADS, A_HDIM)
    else:
        hist_k = cache[0].reshape(b, ATT_REACH, A_WIDTH).astype(BF16)
        hist_v = cache[1].reshape(b, ATT_REACH, A_WIDTH).astype(BF16)
        first_valid_block = 0
        k_out = kv32[..., A_WIDTH:2 * A_WIDTH].reshape(b, s, A_HEADS, A_HDIM)
        v_out = kv32[..., 2 * A_WIDTH:].reshape(b, s, A_HEADS, A_HDIM)
    sq = -(-s // ATT_Q) * ATT_Q
    tail = ((0, 0), (0, sq - s), (0, 0))
    oa = _attn(jnp.pad(q, tail), jnp.pad(jnp.concatenate([hist_k, k], axis=1), tail),
               jnp.pad(jnp.concatenate([hist_v, v], axis=1), tail), lw["bias"], first_valid_block)
    oa = oa[:, :s].reshape(t, A_WIDTH)

    x1 = _mix(hg.reshape(t, M_WIDTH), oa, sig3, x, lw["w_bm"], lw["w_ba"], lw["w_o"], lw["ln1g"], lw["ln1b"])
    scores, aux = _peer_sel(x1, lw["wq_t"], lw["subkeys"])
    x3_out = _peer_ffn(x1, lw["u"], lw["vt"], scores, aux, pe3.reshape(t, PLE_DIM), lw["plp"], lw["plg"],
                       lw["ln2g"], lw["ln2b"])
    return x3_out.reshape(b, s, D_MODEL), (c_new, n_new, m_new), (k_out, v_out)


def kernel(x_prompt, x_sample, cache_attn_k, cache_attn_v, state_mlstm_C, state_mlstm_n, state_mlstm_m,
           p_prompt, p_sample, w_in, mlstm_if_bias, mlstm_norm_w, attn_rel_bias, w_branch_m, w_branch_a,
           w_out, ln1_g, ln1_b, ln2_g, ln2_b, peer_wq, peer_subkeys, peer_u, peer_v, ple_proj, ple_gate):
    layer_w = (w_in, mlstm_if_bias, mlstm_norm_w, attn_rel_bias, w_branch_m, w_branch_a, w_out,
               ln1_g, ln1_b, ln2_g, ln2_b, peer_wq, peer_subkeys, peer_u, peer_v, ple_proj, ple_gate)
    yp, ys = x_prompt, x_sample
    outs_p = [[] for _ in range(5)]
    outs_s = [[] for _ in range(5)]
    for i in range(DEPTH):
        lw = _prep_layer(*(w[i] for w in layer_w))
        yp, (cp, np_, mp), (kp, vp) = _layer(yp, p_prompt[i], lw, None, None)
        ys, (cs, ns, ms), (ks, vs) = _layer(
            ys, p_sample[i], lw, (state_mlstm_C[i], state_mlstm_n[i], state_mlstm_m[i]),
            (cache_attn_k[i], cache_attn_v[i]))
        for lst, val in zip(outs_p, (kp, vp, cp, np_, mp)):
            lst.append(val)
        for lst, val in zip(outs_s, (ks, vs, cs, ns, ms)):
            lst.append(val)
    st = lambda l: jnp.stack(l, axis=0)
    return (yp, ys) + tuple(st(l) for l in outs_p) + tuple(st(l) for l in outs_s)
```

```python
import functools
import math

import jax
import jax.numpy as jnp
from jax import lax
from jax.experimental import pallas as pl
from jax.experimental.pallas import tpu as pltpu

F32 = jnp.float32
BF16 = jnp.bfloat16

D_MODEL = 1024
CHUNK = 64
M_HEADS = 4
M_HDIM = 256
M_WIDTH = 1024
A_HEADS = 8
A_HDIM = 64
A_WIDTH = 512
PAST_CHUNKS = 8
ATT_REACH = 512
REL_CLIP = 128
R_HEADS = 8
N_KEYS = 128
N_EXPERTS = N_KEYS * N_KEYS
HALF_KEY = 128
TOPK = 16
PLE_DIM = 256
DEPTH = 2
ALPHA = (2 * DEPTH) ** 0.25
LN_EPS = 1e-5

LANES = 128
MLSTM_ROWS = 2
ATT_Q = 2 * CHUNK
ATT_WIN = ATT_REACH + ATT_Q
VMEM_LIMIT = 48 * 1024 * 1024
NEG_INF = float("-inf")


def _cparams(sem):
    return pltpu.CompilerParams(dimension_semantics=sem, vmem_limit_bytes=VMEM_LIMIT)


def _sigmoid(x):
    return 1.0 / (1.0 + jnp.exp(-x))


def _dot(a, b):
    return jnp.dot(a, b, preferred_element_type=F32)


def _dot_nt(a, b):
    return lax.dot_general(a, b, (((1,), (1,)), ((), ())), preferred_element_type=F32)


def _dot_tn(a, b):
    return lax.dot_general(a, b, (((0,), (0,)), ((), ())), preferred_element_type=F32)


def _layer_norm(y, g, b):
    mu = jnp.mean(y, axis=-1, keepdims=True)
    d = y - mu
    var = jnp.mean(d * d, axis=-1, keepdims=True)
    return d * lax.rsqrt(var + LN_EPS) * g + b


def _proj_body(x_ref, w_ref, s_ref, *o_refs, act):
    acc = _dot(x_ref[...].astype(BF16), w_ref[...]) * s_ref[...]
    if act == "sigmoid":
        acc = _sigmoid(acc)
    for o in o_refs:
        o[...] = acc.astype(o.dtype)


def _proj(x, w, scale, act, out_dtypes, tb=512, nb=512):
    t, k = x.shape
    n = w.shape[1]
    return pl.pallas_call(
        functools.partial(_proj_body, act=act),
        grid=(t // tb, n // nb),
        in_specs=[pl.BlockSpec((tb, k), lambda i, j: (i, 0)),
                  pl.BlockSpec((k, nb), lambda i, j: (0, j)),
                  pl.BlockSpec((1, nb), lambda i, j: (0, j))],
        out_specs=[pl.BlockSpec((tb, nb), lambda i, j: (i, j)) for _ in out_dtypes],
        out_shape=[jax.ShapeDtypeStruct((t, n), d) for d in out_dtypes],
        compiler_params=_cparams(("parallel", "arbitrary")),
    )(x, w, scale)


def _split3(v):
    hi = v.astype(BF16)
    r = v - hi.astype(F32)
    mid = r.astype(BF16)
    lo = (r - mid.astype(F32)).astype(BF16)
    return hi, mid, lo


def _gate_body(x_ref, w_ref, b_ref, o_ref):
    xh, xm, xl = _split3(x_ref[...])
    wh, wm, wl = _split3(w_ref[...])
    z = (_dot(xh, wh) + (_dot(xh, wm) + _dot(xm, wh))
         + (_dot(xh, wl) + _dot(xm, wm) + _dot(xl, wh))) + b_ref[...]
    col = lax.broadcasted_iota(jnp.int32, z.shape, 1)
    log_sig = jnp.minimum(z, 0.0) - jnp.log1p(jnp.exp(-jnp.abs(z)))
    o_ref[...] = jnp.where(col < M_HEADS, z, log_sig)


def _gates(x, w_if, b_if, tb=512):
    t, k = x.shape
    return pl.pallas_call(
        _gate_body,
        grid=(t // tb,),
        in_specs=[pl.BlockSpec((tb, k), lambda i: (i, 0)),
                  pl.BlockSpec((k, LANES), lambda i: (0, 0)),
                  pl.BlockSpec((1, LANES), lambda i: (0, 0))],
        out_specs=pl.BlockSpec((tb, LANES), lambda i: (i, 0)),
        out_shape=jax.ShapeDtypeStruct((t, LANES), F32),
        compiler_params=_cparams(("parallel",)),
    )(x, w_if, b_if)


def _mlstm_body(qkv_ref, g_ref, og_ref, nw_ref, c0_ref, n0_ref, m0_ref,
                h_ref, c_ref, n_ref, m_ref):
    @pl.when(pl.program_id(1) == 0)
    def _():
        c_ref[...] = c0_ref[...]
        n_ref[...] = n0_ref[...]
        m_ref[...] = m0_ref[...]

    for bb in range(qkv_ref.shape[0]):
        _mlstm_chunk_update(bb, qkv_ref, g_ref, og_ref, nw_ref, h_ref, c_ref, n_ref, m_ref)


def _mlstm_chunk_update(bb, qkv_ref, g_ref, og_ref, nw_ref, h_ref, c_ref, n_ref, m_ref):
    L = CHUNK
    g = g_ref[bb]
    gt = jnp.concatenate([g, jnp.zeros((LANES - L, LANES), F32)], axis=0).T
    t_i = lax.broadcasted_iota(jnp.int32, (L, L), 0)
    s_i = lax.broadcasted_iota(jnp.int32, (L, L), 1)
    causal = s_i <= t_i
    lane = lax.broadcasted_iota(jnp.int32, (1, LANES), 1)
    m_all = m_ref[bb]
    m_next = m_all
    for h in range(M_HEADS):
        sl = slice(h * M_HDIM, (h + 1) * M_HDIM)
        q = qkv_ref[bb, :, h * M_HDIM:(h + 1) * M_HDIM]
        k = qkv_ref[bb, :, M_WIDTH + h * M_HDIM:M_WIDTH + (h + 1) * M_HDIM]
        v = qkv_ref[bb, :, 2 * M_WIDTH + h * M_HDIM:2 * M_WIDTH + (h + 1) * M_HDIM]
        ig_col = g[:, h:h + 1]
        lf_col = g[:, M_HEADS + h:M_HEADS + h + 1]
        ig_row = gt[h:h + 1, :L]
        lf_row = gt[M_HEADS + h:M_HEADS + h + 1, :L]
        b_col = jnp.sum(jnp.where(causal, lf_row, 0.0), axis=1, keepdims=True)
        b_row = jnp.sum(jnp.where(t_i <= s_i, lf_col, 0.0), axis=0, keepdims=True)
        m_prev = jnp.sum(jnp.where(lane == h, m_all, 0.0), axis=1, keepdims=True)
        inter = b_col + m_prev
        dmat = jnp.where(causal, b_col - b_row + ig_row, NEG_INF)
        m_t = jnp.maximum(inter, jnp.max(dmat, axis=1, keepdims=True))
        w_inter = jnp.exp(inter - m_t)
        w_intra = jnp.exp(dmat - m_t)
        a = w_intra * _dot_nt(q, k)
        c_h = c_ref[bb, h]
        n_h = n_ref[bb, :, sl]
        qf = q.astype(F32)
        num = w_inter * _dot_nt(q, c_h.astype(BF16)) + _dot(a.astype(BF16), v)
        den = w_inter * jnp.sum(qf * n_h, axis=1, keepdims=True) + jnp.sum(a, axis=1, keepdims=True)
        hh = num * (1.0 / jnp.maximum(jnp.abs(den), jnp.exp(-m_t)))
        mu = jnp.mean(hh, axis=1, keepdims=True)
        dv = hh - mu
        var = jnp.mean(dv * dv, axis=1, keepdims=True)
        hn = dv * lax.rsqrt(var + LN_EPS) * nw_ref[:, sl]
        h_ref[bb, :, sl] = (og_ref[bb, :, sl].astype(F32) * hn).astype(h_ref.dtype)
        m_new = m_t[L - 1:L, :]
        b_last = b_col[L - 1:L, :]
        g_state = jnp.exp(b_last + m_prev - m_new)
        g_s = jnp.exp(b_last - b_col + ig_col - m_new)
        vg = (g_s * v.astype(F32)).astype(BF16)
        c_ref[bb, h] = g_state * c_h + _dot_tn(vg, k)
        n_ref[bb, :, sl] = g_state * n_h + jnp.sum(g_s * k.astype(F32), axis=0, keepdims=True)
        m_next = jnp.where(lane == h, m_new, m_next)
    m_ref[bb] = m_next


def _mlstm(qkv, gates, og3, norm_w, c0, n0, m0):
    b, s, _ = qkv.shape
    nc = s // CHUNK
    nb = MLSTM_ROWS
    return pl.pallas_call(
        _mlstm_body,
        grid=(b // nb, nc),
        in_specs=[pl.BlockSpec((nb, CHUNK, 3 * M_WIDTH), lambda i, c: (i, c, 0)),
                  pl.BlockSpec((nb, CHUNK, LANES), lambda i, c: (i, c, 0)),
                  pl.BlockSpec((nb, CHUNK, M_WIDTH), lambda i, c: (i, c, 0)),
                  pl.BlockSpec((1, M_WIDTH), lambda i, c: (0, 0)),
                  pl.BlockSpec((nb, M_HEADS, M_HDIM, M_HDIM), lambda i, c: (i, 0, 0, 0)),
                  pl.BlockSpec((nb, 1, M_WIDTH), lambda i, c: (i, 0, 0)),
                  pl.BlockSpec((nb, 1, LANES), lambda i, c: (i, 0, 0))],
        out_specs=[pl.BlockSpec((nb, CHUNK, M_WIDTH), lambda i, c: (i, c, 0)),
                   pl.BlockSpec((nb, M_HEADS, M_HDIM, M_HDIM), lambda i, c: (i, 0, 0, 0)),
                   pl.BlockSpec((nb, 1, M_WIDTH), lambda i, c: (i, 0, 0)),
                   pl.BlockSpec((nb, 1, LANES), lambda i, c: (i, 0, 0))],
        out_shape=[jax.ShapeDtypeStruct((b, s, M_WIDTH), BF16),
                   jax.ShapeDtypeStruct((b, M_HEADS, M_HDIM, M_HDIM), F32),
                   jax.ShapeDtypeStruct((b, 1, M_WIDTH), F32),
                   jax.ShapeDtypeStruct((b, 1, LANES), F32)],
        compiler_params=_cparams(("parallel", "arbitrary")),
    )(qkv, gates, og3, norm_w, c0, n0, m0)


def _attn_body(q_ref, k_ref, v_ref, bias_ref, o_ref, *, first_valid_block):
    p = pl.program_id(1)
    row0 = pl.multiple_of(p * ATT_Q, ATT_Q)
    kw = k_ref[0, pl.ds(row0, ATT_WIN), :]
    vw = v_ref[0, pl.ds(row0, ATT_WIN), :]
    key_j = lax.broadcasted_iota(jnp.int32, (ATT_Q, ATT_WIN), 1)
    key_ok = key_j >= (first_valid_block - p) * ATT_Q
    lane = lax.broadcasted_iota(jnp.int32, (1, LANES), 1)
    for pair in range(A_HEADS // 2):
        cs = slice(pair * LANES, (pair + 1) * LANES)
        qp = q_ref[0, :, cs]
        kp = kw[:, cs]
        vp = vw[:, cs]
        o_pair = jnp.zeros((ATT_Q, LANES), F32)
        for e in range(2):
            head = 2 * pair + e
            in_head = (lane >= e * A_HDIM) & (lane < (e + 1) * A_HDIM)
            qm = jnp.where(in_head, qp, jnp.zeros_like(qp))
            s = _dot_nt(qm, kp) * (A_HDIM ** -0.5) + bias_ref[head]
            s = jnp.where(key_ok, s, NEG_INF)
            m = jnp.max(s, axis=1, keepdims=True)
            pe = jnp.exp(s - m)
            l = jnp.sum(pe, axis=1, keepdims=True)
            vm = jnp.where(in_head, vp, jnp.zeros_like(vp))
            o_pair = o_pair + _dot(pe.astype(BF16), vm) * (1.0 / l)
        o_ref[0, :, cs] = o_pair.astype(o_ref.dtype)


def _attn(q, k_pad, v_pad, bias, first_valid_block):
    b, sq, _ = q.shape
    sk = k_pad.shape[1]
    return pl.pallas_call(
        functools.partial(_attn_body, first_valid_block=first_valid_block),
        grid=(b, sq // ATT_Q),
        in_specs=[pl.BlockSpec((1, ATT_Q, A_WIDTH), lambda i, p: (i, p, 0)),
                  pl.BlockSpec((1, sk, A_WIDTH), lambda i, p: (i, 0, 0)),
                  pl.BlockSpec((1, sk, A_WIDTH), lambda i, p: (i, 0, 0)),
                  pl.BlockSpec((A_HEADS, ATT_Q, ATT_WIN), lambda i, p: (0, 0, 0))],
        out_specs=pl.BlockSpec((1, ATT_Q, A_WIDTH), lambda i, p: (i, p, 0)),
        out_shape=jax.ShapeDtypeStruct((b, sq, A_WIDTH), BF16),
        compiler_params=_cparams(("parallel", "arbitrary")),
    )(q, k_pad, v_pad, bias)


def _attn_bias(rel_bias):
    i = jnp.arange(ATT_Q)[:, None]
    j = jnp.arange(ATT_WIN)[None, :]
    n_diag = ATT_Q + ATT_WIN
    k = jnp.arange(n_diag)
    e = rel_bias.astype(F32)[:, jnp.clip(ATT_REACH + ATT_Q - 1 - k, -REL_CLIP, REL_CLIP) + REL_CLIP]
    skew = jnp.tile(e, (1, ATT_Q))[:, :ATT_Q * (n_diag - 1)].reshape(A_HEADS, ATT_Q, n_diag - 1)
    bias = skew[:, :, ATT_Q - 1:ATT_Q - 1 + ATT_WIN]
    qc = i // CHUNK
    kc = j // CHUNK - PAST_CHUNKS
    visible = (kc <= qc) & (kc >= qc - PAST_CHUNKS)
    return jnp.where(visible[None], bias, NEG_INF)


def _mix_body(hg_ref, oa_ref, gm_ref, ga_ref, x_ref, wbm_ref, wba_ref, wo_ref, g_ref, b_ref, o_ref):
    ym = _dot(hg_ref[...], wbm_ref[...])
    ya = _dot(oa_ref[...], wba_ref[...])
    gated = gm_ref[...].astype(F32) * ym + ga_ref[...].astype(F32) * ya
    mix = _dot(gated.astype(BF16), wo_ref[...])
    o_ref[...] = _layer_norm(ALPHA * x_ref[...] + mix, g_ref[...], b_ref[...])


def _mix(hg, oa, sg3, x, wbm, wba, wo, ln_g, ln_b, tb=512):
    t = x.shape[0]
    row = lambda i: (i, 0)
    const = lambda i: (0, 0)
    return pl.pallas_call(
        _mix_body,
        grid=(t // tb,),
        in_specs=[pl.BlockSpec((tb, M_WIDTH), row),
                  pl.BlockSpec((tb, A_WIDTH), row),
                  pl.BlockSpec((tb, D_MODEL), lambda i: (i, 1)),
                  pl.BlockSpec((tb, D_MODEL), lambda i: (i, 2)),
                  pl.BlockSpec((tb, D_MODEL), row),
                  pl.BlockSpec((M_WIDTH, D_MODEL), const),
                  pl.BlockSpec((A_WIDTH, D_MODEL), const),
                  pl.BlockSpec((D_MODEL, D_MODEL), const),
                  pl.BlockSpec((1, D_MODEL), const),
                  pl.BlockSpec((1, D_MODEL), const)],
        out_specs=pl.BlockSpec((tb, D_MODEL), row),
        out_shape=jax.ShapeDtypeStruct((t, D_MODEL), F32),
        compiler_params=_cparams(("parallel",)),
    )(hg, oa, sg3, sg3, x, wbm, wba, wo, ln_g, ln_b)


SUBLANES = 8
LOG2E = math.log2(math.e)


def _batcher_network(lo, hi):
    def merge(lo, hi, r):
        step = r * 2
        if step < hi - lo:
            yield from merge(lo, hi, step)
            yield from merge(lo + r, hi, step)
            yield from [(i, i + r) for i in range(lo + r, hi - r, step)]
        else:
            yield (lo, lo + r)
    if hi - lo >= 1:
        mid = lo + (hi - lo) // 2
        yield from _batcher_network(lo, mid)
        yield from _batcher_network(mid + 1, hi)
        yield from merge(lo, hi, 1)


_SORT16 = tuple(_batcher_network(0, TOPK - 1))
_BITONIC16 = tuple((i, i + s) for s in (8, 4, 2, 1) for i in range(TOPK) if (i // s) % 2 == 0)


def _compare_exchange(rows, net):
    rows = list(rows)
    for i, j in net:
        a, b = rows[i], rows[j]
        if b is None:
            continue
        if a is None:
            rows[i], rows[j] = b, None
        else:
            rows[i], rows[j] = jnp.maximum(a, b), jnp.minimum(a, b)
    return rows


def _top16_values(slabs):
    rows = list(slabs) + [None] * (TOPK - len(slabs))
    rows = _compare_exchange(rows, _SORT16)
    for shift in (4, 2, 1):
        merged = []
        for r in range(TOPK):
            a, b = rows[r], rows[TOPK - 1 - r]
            b = None if b is None else pltpu.roll(b, shift, axis=0)
            merged.append(b if a is None else a if b is None else jnp.maximum(a, b))
        rows = _compare_exchange(merged, _BITONIC16)
    return rows


def _by_sublane(rows, sub):
    out = rows[SUBLANES - 1]
    for g in range(SUBLANES - 2, -1, -1):
        out = jnp.where(sub == g, rows[g], out)
    return out


def _peer_sel_body(x_ref, wq_ref, sk_ref, s_ref, aux_ref):
    tb = x_ref.shape[0]
    qt = _dot_nt(wq_ref[...], x_ref[...].astype(BF16))
    sub = lax.broadcasted_iota(jnp.int32, (SUBLANES, tb), 0)
    for h in range(R_HEADS):
        top = []
        for p in range(2):
            r = h * 2 + p
            qhp = qt[r * HALF_KEY:(r + 1) * HALF_KEY, :].astype(BF16)
            st = _dot(sk_ref[r], qhp) * LOG2E
            s_ref[r] = st
            top.append(_top16_values([st[g * SUBLANES:(g + 1) * SUBLANES, :] for g in range(N_KEYS // SUBLANES)]))
        c1, c2 = top
        v1_lo, v1_hi = _by_sublane(c1[:SUBLANES], sub), _by_sublane(c1[SUBLANES:], sub)
        v2_hi = _by_sublane(c2[SUBLANES:], sub)
        firsts = [v1_lo] * SUBLANES + [c1[0], v1_hi]
        seconds = c2[:SUBLANES] + [v2_hi, c2[0]]
        cand = [f + s for f, s in zip(firsts, seconds)]
        cv = _top16_values(cand)
        tau, best = cv[TOPK - 1], cv[0]
        picked = [c >= tau for c in cand]
        z = sum(jnp.where(pk, jnp.exp2(c - best), 0.0) for pk, c in zip(picked, cand))
        shift = best[0:1, :] + jnp.log2(jnp.sum(z, axis=0, keepdims=True))
        shifted = [jnp.where(pk, (f - shift) + s, jnp.inf) for pk, f, s in zip(picked, firsts, seconds)]
        tau_shifted = functools.reduce(jnp.minimum, shifted)
        s_ref[2 * h] = s_ref[2 * h] - shift
        aux_ref[h:h + 1, :] = jnp.min(tau_shifted, axis=0, keepdims=True)


def _peer_sel(x, wq_t, subkeys, tb=256):
    t = x.shape[0]
    return pl.pallas_call(
        _peer_sel_body,
        grid=(t // tb,),
        in_specs=[pl.BlockSpec((tb, D_MODEL), lambda i: (i, 0)),
                  pl.BlockSpec((2 * R_HEADS * HALF_KEY, D_MODEL), lambda i: (0, 0)),
                  pl.BlockSpec((2 * R_HEADS, N_KEYS, HALF_KEY), lambda i: (0, 0, 0))],
        out_specs=[pl.BlockSpec((2 * R_HEADS, N_KEYS, tb), lambda i: (0, 0, i)),
                   pl.BlockSpec((R_HEADS, tb), lambda i: (0, i))],
        out_shape=[jax.ShapeDtypeStruct((2 * R_HEADS, N_KEYS, t), F32),
                   jax.ShapeDtypeStruct((R_HEADS, t), F32)],
        compiler_params=_cparams(("parallel",)),
    )(x, wq_t, subkeys)


def _gelu_tanh(x):
    return 0.5 * x * (1.0 + jnp.tanh(math.sqrt(2.0 / math.pi) * (x + 0.044715 * (x * x * x))))


PEER_TILE = (SUBLANES // 2) * N_KEYS
PEER_TILES = N_EXPERTS // PEER_TILE


def _peer_weighted_act(st_ref, p_ref, s_ref, aux_ref, i0, row0, tb):
    for ii in range(PEER_TILE // N_KEYS):
        rs = slice(ii * N_KEYS, (ii + 1) * N_KEYS)
        for lg in range(tb // LANES):
            ls = slice(lg * LANES, (lg + 1) * LANES)
            w = jnp.zeros((N_KEYS, LANES), F32)
            for h in range(R_HEADS):
                s1 = s_ref[2 * h, pl.ds(i0, SUBLANES), ls][row0 + ii:row0 + ii + 1, :]
                s = s1 + s_ref[2 * h + 1, :, ls]
                w = w + jnp.where(s >= aux_ref[h:h + 1, ls], jnp.exp2(s), 0.0)
            p_ref[rs, ls] = _gelu_tanh(st_ref[rs, ls]) * w.astype(BF16)


def _peer_ffn_body(x_ref, u0_ref, ua_ref, ub_ref, vta_ref, vtb_ref, vtl_ref, s_ref, aux_ref,
                   pe_ref, plp_ref, plg_ref, g_ref, b_ref, o_ref,
                   xb_scr, acc_scr, st0, st1, p0, p1):
    step = pl.program_id(1)
    tb = x_ref.shape[0]
    keys = PEER_TILE // N_KEYS

    @pl.when(step == 0)
    def _():
        xb_scr[...] = x_ref[...].astype(BF16)
        acc_scr[...] = jnp.zeros_like(acc_scr)
        st0[...] = _dot_nt(u0_ref[...], xb_scr[...]).astype(BF16)
        p1[...] = jnp.zeros_like(p1)

    i0 = pl.multiple_of(step * SUBLANES, SUBLANES)
    acc_scr[...] += _dot(vta_ref[0], p1[...])
    st1[...] = _dot_nt(ua_ref[...], xb_scr[...]).astype(BF16)
    _peer_weighted_act(st0, p0, s_ref, aux_ref, i0, 0, tb)
    acc_scr[...] += _dot(vtb_ref[0], p0[...])
    st0[...] = _dot_nt(ub_ref[...], xb_scr[...]).astype(BF16)
    _peer_weighted_act(st1, p1, s_ref, aux_ref, i0, keys, tb)

    @pl.when(step == pl.num_programs(1) - 1)
    def _():
        acc = acc_scr[...] + _dot(vtl_ref[0], p1[...])
        x = x_ref[...]
        y = _layer_norm(ALPHA * x + acc.T, g_ref[...], b_ref[...])
        ple = _dot(pe_ref[...].astype(BF16), plp_ref[...])
        gate = _sigmoid(_dot(y.astype(BF16), plg_ref[...]))
        o_ref[...] = y + ple * gate


def _peer_ffn(x, u, vt_tiles, scores, aux, pe, plp, plg, ln_g, ln_b, tb=512):
    t = x.shape[0]
    row = lambda i, e: (i, 0)
    const = lambda i, e: (0, 0)
    last = PEER_TILES - 1
    return pl.pallas_call(
        _peer_ffn_body,
        grid=(t // tb, PEER_TILES // 2),
        in_specs=[pl.BlockSpec((tb, D_MODEL), row),
                  pl.BlockSpec((PEER_TILE, D_MODEL), lambda i, e: (0, 0)),
                  pl.BlockSpec((PEER_TILE, D_MODEL), lambda i, e: (2 * e + 1, 0)),
                  pl.BlockSpec((PEER_TILE, D_MODEL), lambda i, e: (jnp.minimum(2 * e + 2, last), 0)),
                  pl.BlockSpec((1, D_MODEL, PEER_TILE), lambda i, e: (jnp.maximum(2 * e - 1, 0), 0, 0)),
                  pl.BlockSpec((1, D_MODEL, PEER_TILE), lambda i, e: (2 * e, 0, 0)),
                  pl.BlockSpec((1, D_MODEL, PEER_TILE), lambda i, e: (last, 0, 0)),
                  pl.BlockSpec((2 * R_HEADS, N_KEYS, tb), lambda i, e: (0, 0, i)),
                  pl.BlockSpec((R_HEADS, tb), lambda i, e: (0, i)),
                  pl.BlockSpec((tb, PLE_DIM), row),
                  pl.BlockSpec((PLE_DIM, D_MODEL), const),
                  pl.BlockSpec((D_MODEL, D_MODEL), const),
                  pl.BlockSpec((1, D_MODEL), const),
                  pl.BlockSpec((1, D_MODEL), const)],
        out_specs=pl.BlockSpec((tb, D_MODEL), row),
        out_shape=jax.ShapeDtypeStruct((t, D_MODEL), F32),
        scratch_shapes=[pltpu.VMEM((tb, D_MODEL), BF16),
                        pltpu.VMEM((D_MODEL, tb), F32),
                        pltpu.VMEM((PEER_TILE, tb), BF16), pltpu.VMEM((PEER_TILE, tb), BF16),
                        pltpu.VMEM((PEER_TILE, tb), BF16), pltpu.VMEM((PEER_TILE, tb), BF16)],
        compiler_params=_cparams(("parallel", "arbitrary")),
    )(x, u, u, u, vt_tiles, vt_tiles, vt_tiles, scores, aux, pe, plp, plg, ln_g, ln_b)


def _prep_layer(w_in, if_bias, norm_w, rel_bias, w_bm, w_ba, w_o, ln1g, ln1b, ln2g, ln2b,
                pq, psk, pu, pv, plp, plg):
    c = 0
    cols = {}
    for name, width in (("mq", M_WIDTH), ("mk", M_WIDTH), ("mv", M_WIDTH), ("mi", M_HEADS), ("mf", M_HEADS),
                        ("mo", M_WIDTH), ("aq", A_WIDTH), ("ak", A_WIDTH), ("av", A_WIDTH),
                        ("gm", D_MODEL), ("ga", D_MODEL)):
        cols[name] = w_in[:, c:c + width]
        c += width
    cat = lambda names: jnp.concatenate([cols[n] for n in names], axis=1)
    ones = lambda n: jnp.ones((1, n), F32)
    w_if = jnp.pad(cat(("mi", "mf")), ((0, 0), (0, LANES - 2 * M_HEADS)))
    b_if = jnp.pad(if_bias.reshape(1, 2 * M_HEADS).astype(F32), ((0, 0), (0, LANES - 2 * M_HEADS)))
    return dict(
        w_mqkv=cat(("mq", "mk", "mv")).astype(BF16),
        s_mqkv=jnp.concatenate([ones(M_WIDTH), ones(M_WIDTH) * (M_HDIM ** -0.5), ones(M_WIDTH)], axis=1),
        w_sig=cat(("mo", "gm", "ga")).astype(BF16), s_sig=ones(3 * D_MODEL),
        w_aqkv=cat(("aq", "ak", "av")).astype(BF16), s_aqkv=ones(3 * A_WIDTH),
        w_if=w_if, b_if=b_if,
        norm_w=norm_w.reshape(1, M_WIDTH).astype(F32),
        bias=_attn_bias(rel_bias),
        w_bm=w_bm.astype(BF16), w_ba=w_ba.astype(BF16), w_o=w_o.astype(BF16),
        ln1g=ln1g.reshape(1, D_MODEL), ln1b=ln1b.reshape(1, D_MODEL),
        ln2g=ln2g.reshape(1, D_MODEL), ln2b=ln2b.reshape(1, D_MODEL),
        wq_t=pq.T.astype(BF16),
        subkeys=psk.reshape(2 * R_HEADS, N_KEYS, HALF_KEY).astype(BF16),
        u=pu.astype(BF16),
        vt=pv.reshape(PEER_TILES, PEER_TILE, D_MODEL).transpose(0, 2, 1).astype(BF16),
        plp=plp.astype(BF16), plg=plg.astype(BF16),
    )


def _layer(x3, pe3, lw, state, cache):
    b, s, _ = x3.shape
    t = b * s
    x = x3.reshape(t, D_MODEL)
    mqkv, = _proj(x, lw["w_mqkv"], lw["s_mqkv"], None, (BF16,), tb=1024, nb=1024)
    sig3, = _proj(x, lw["w_sig"], lw["s_sig"], "sigmoid", (BF16,), tb=1024, nb=1024)
    aqkv, akv32 = _proj(x, lw["w_aqkv"], lw["s_aqkv"], None, (BF16, F32), tb=1024, nb=512)
    gates = _gates(x, lw["w_if"], lw["b_if"])

    if state is None:
        c0 = jnp.zeros((b, M_HEADS, M_HDIM, M_HDIM), F32)
        n0 = jnp.zeros((b, 1, M_WIDTH), F32)
        m0 = jnp.zeros((b, 1, LANES), F32)
    else:
        c0 = state[0].astype(F32)
        n0 = state[1].astype(F32).reshape(b, 1, M_WIDTH)
        m0 = jnp.pad(state[2].astype(F32), ((0, 0), (0, LANES - M_HEADS))).reshape(b, 1, LANES)
    hg, c_new, n_new, m_new = _mlstm(mqkv.reshape(b, s, 3 * M_WIDTH), gates.reshape(b, s, LANES),
                                     sig3.reshape(b, s, 3 * D_MODEL), lw["norm_w"], c0, n0, m0)
    n_new = n_new.reshape(b, M_HEADS, M_HDIM)
    m_new = m_new.reshape(b, LANES)[:, :M_HEADS]

    a3 = aqkv.reshape(b, s, 3 * A_WIDTH)
    q, k, v = a3[..., :A_WIDTH], a3[..., A_WIDTH:2 * A_WIDTH], a3[..., 2 * A_WIDTH:]
    kv32 = akv32.reshape(b, s, 3 * A_WIDTH)
    if cache is None:
        hist_k = jnp.zeros((b, ATT_REACH, A_WIDTH), BF16)
        hist_v = hist_k
        first_valid_block = ATT_REACH // ATT_Q
        keep = min(ATT_REACH, s)
        k_out = kv32[:, s - keep:, A_WIDTH:2 * A_WIDTH].reshape(b, keep, A_HEADS, A_HDIM)
        v_out = kv32[:, s - keep:, 2 * A_WIDTH:].reshape(b, keep, A_HEADS, A_HDIM)
    else:
        hist_k = cache[0].reshape(b, ATT_REACH, A_WIDTH).astype(BF16)
        hist_v = cache[1].reshape(b, ATT_REACH, A_WIDTH).astype(BF16)
        first_valid_block = 0
        k_out = kv32[..., A_WIDTH:2 * A_WIDTH].reshape(b, s, A_HEADS, A_HDIM)
        v_out = kv32[..., 2 * A_WIDTH:].reshape(b, s, A_HEADS, A_HDIM)
    sq = -(-s // ATT_Q) * ATT_Q
    tail = ((0, 0), (0, sq - s), (0, 0))
    oa = _attn(jnp.pad(q, tail), jnp.pad(jnp.concatenate([hist_k, k], axis=1), tail),
               jnp.pad(jnp.concatenate([hist_v, v], axis=1), tail), lw["bias"], first_valid_block)
    oa = oa[:, :s].reshape(t, A_WIDTH)

    x1 = _mix(hg.reshape(t, M_WIDTH), oa, sig3, x, lw["w_bm"], lw["w_ba"], lw["w_o"], lw["ln1g"], lw["ln1b"])
    scores, aux = _peer_sel(x1, lw["wq_t"], lw["subkeys"])
    x3_out = _peer_ffn(x1, lw["u"], lw["vt"], scores, aux, pe3.reshape(t, PLE_DIM), lw["plp"], lw["plg"],
                       lw["ln2g"], lw["ln2b"])
    return x3_out.reshape(b, s, D_MODEL), (c_new, n_new, m_new), (k_out, v_out)


def kernel(x_prompt, x_sample, cache_attn_k, cache_attn_v, state_mlstm_C, state_mlstm_n, state_mlstm_m,
           p_prompt, p_sample, w_in, mlstm_if_bias, mlstm_norm_w, attn_rel_bias, w_branch_m, w_branch_a,
           w_out, ln1_g, ln1_b, ln2_g, ln2_b, peer_wq, peer_subkeys, peer_u, peer_v, ple_proj, ple_gate):
    layer_w = (w_in, mlstm_if_bias, mlstm_norm_w, attn_rel_bias, w_branch_m, w_branch_a, w_out,
               ln1_g, ln1_b, ln2_g, ln2_b, peer_wq, peer_subkeys, peer_u, peer_v, ple_proj, ple_gate)
    yp, ys = x_prompt, x_sample
    outs_p = [[] for _ in range(5)]
    outs_s = [[] for _ in range(5)]
    for i in range(DEPTH):
        lw = _prep_layer(*(w[i] for w in layer_w))
        yp, (cp, np_, mp), (kp, vp) = _layer(yp, p_prompt[i], lw, None, None)
        ys, (cs, ns, ms), (ks, vs) = _layer(
            ys, p_sample[i], lw, (state_mlstm_C[i], state_mlstm_n[i], state_mlstm_m[i]),
            (cache_attn_k[i], cache_attn_v[i]))
        for lst, val in zip(outs_p, (kp, vp, cp, np_, mp)):
            lst.append(val)
        for lst, val in zip(outs_s, (ks, vs, cs, ns, ms)):
            lst.append(val)
    st = lambda l: jnp.stack(l, axis=0)
    return (yp, ys) + tuple(st(l) for l in outs_p) + tuple(st(l) for l in outs_s)
```

```python
import functools
import math

import jax
import jax.numpy as jnp
from jax import lax
from jax.experimental import pallas as pl
from jax.experimental.pallas import tpu as pltpu

F32 = jnp.float32
BF16 = jnp.bfloat16

D_MODEL = 1024
CHUNK = 64
M_HEADS = 4
M_HDIM = 256
M_WIDTH = 1024
A_HEADS = 8
A_HDIM = 64
A_WIDTH = 512
PAST_CHUNKS = 8
ATT_REACH = 512
REL_CLIP = 128
R_HEADS = 8
N_KEYS = 128
N_EXPERTS = N_KEYS * N_KEYS
HALF_KEY = 128
TOPK = 16
PLE_DIM = 256
DEPTH = 2
ALPHA = (2 * DEPTH) ** 0.25
LN_EPS = 1e-5

LANES = 128
MLSTM_ROWS = 2
ATT_Q = 2 * CHUNK
ATT_WIN = ATT_REACH + ATT_Q
VMEM_LIMIT = 48 * 1024 * 1024
NEG_INF = float("-inf")


def _cparams(sem):
    return pltpu.CompilerParams(dimension_semantics=sem, vmem_limit_bytes=VMEM_LIMIT)


def _sigmoid(x):
    return 1.0 / (1.0 + jnp.exp(-x))


def _dot(a, b):
    return jnp.dot(a, b, preferred_element_type=F32)


def _dot_nt(a, b):
    return lax.dot_general(a, b, (((1,), (1,)), ((), ())), preferred_element_type=F32)


def _dot_tn(a, b):
    return lax.dot_general(a, b, (((0,), (0,)), ((), ())), preferred_element_type=F32)


def _layer_norm(y, g, b):
    mu = jnp.mean(y, axis=-1, keepdims=True)
    d = y - mu
    var = jnp.mean(d * d, axis=-1, keepdims=True)
    return d * lax.rsqrt(var + LN_EPS) * g + b


def _proj_body(x_ref, w_ref, s_ref, *o_refs, act):
    acc = _dot(x_ref[...].astype(BF16), w_ref[...]) * s_ref[...]
    if act == "sigmoid":
        acc = _sigmoid(acc)
    for o in o_refs:
        o[...] = acc.astype(o.dtype)


def _proj(x, w, scale, act, out_dtypes, tb=512, nb=512):
    t, k = x.shape
    n = w.shape[1]
    return pl.pallas_call(
        functools.partial(_proj_body, act=act),
        grid=(t // tb, n // nb),
        in_specs=[pl.BlockSpec((tb, k), lambda i, j: (i, 0)),
                  pl.BlockSpec((k, nb), lambda i, j: (0, j)),
                  pl.BlockSpec((1, nb), lambda i, j: (0, j))],
        out_specs=[pl.BlockSpec((tb, nb), lambda i, j: (i, j)) for _ in out_dtypes],
        out_shape=[jax.ShapeDtypeStruct((t, n), d) for d in out_dtypes],
        compiler_params=_cparams(("parallel", "arbitrary")),
    )(x, w, scale)


def _split3(v):
    hi = v.astype(BF16)
    r = v - hi.astype(F32)
    mid = r.astype(BF16)
    lo = (r - mid.astype(F32)).astype(BF16)
    return hi, mid, lo


def _gate_body(x_ref, w_ref, b_ref, o_ref):
    xh, xm, xl = _split3(x_ref[...])
    wh, wm, wl = _split3(w_ref[...])
    z = (_dot(xh, wh) + (_dot(xh, wm) + _dot(xm, wh))
         + (_dot(xh, wl) + _dot(xm, wm) + _dot(xl, wh))) + b_ref[...]
    col = lax.broadcasted_iota(jnp.int32, z.shape, 1)
    log_sig = jnp.minimum(z, 0.0) - jnp.log1p(jnp.exp(-jnp.abs(z)))
    o_ref[...] = jnp.where(col < M_HEADS, z, log_sig)


def _gates(x, w_if, b_if, tb=512):
    t, k = x.shape
    return pl.pallas_call(
        _gate_body,
        grid=(t // tb,),
        in_specs=[pl.BlockSpec((tb, k), lambda i: (i, 0)),
                  pl.BlockSpec((k, LANES), lambda i: (0, 0)),
                  pl.BlockSpec((1, LANES), lambda i: (0, 0))],
        out_specs=pl.BlockSpec((tb, LANES), lambda i: (i, 0)),
        out_shape=jax.ShapeDtypeStruct((t, LANES), F32),
        compiler_params=_cparams(("parallel",)),
    )(x, w_if, b_if)


def _mlstm_body(qkv_ref, g_ref, og_ref, nw_ref, c0_ref, n0_ref, m0_ref,
                h_ref, c_ref, n_ref, m_ref):
    @pl.when(pl.program_id(1) == 0)
    def _():
        c_ref[...] = c0_ref[...]
        n_ref[...] = n0_ref[...]
        m_ref[...] = m0_ref[...]

    for bb in range(qkv_ref.shape[0]):
        _mlstm_chunk_update(bb, qkv_ref, g_ref, og_ref, nw_ref, h_ref, c_ref, n_ref, m_ref)


def _mlstm_chunk_update(bb, qkv_ref, g_ref, og_ref, nw_ref, h_ref, c_ref, n_ref, m_ref):
    L = CHUNK
    g = g_ref[bb]
    gt = jnp.concatenate([g, jnp.zeros((LANES - L, LANES), F32)], axis=0).T
    t_i = lax.broadcasted_iota(jnp.int32, (L, L), 0)
    s_i = lax.broadcasted_iota(jnp.int32, (L, L), 1)
    causal = s_i <= t_i
    lane = lax.broadcasted_iota(jnp.int32, (1, LANES), 1)
    m_all = m_ref[bb]
    m_next = m_all
    for h in range(M_HEADS):
        sl = slice(h * M_HDIM, (h + 1) * M_HDIM)
        q = qkv_ref[bb, :, h * M_HDIM:(h + 1) * M_HDIM]
        k = qkv_ref[bb, :, M_WIDTH + h * M_HDIM:M_WIDTH + (h + 1) * M_HDIM]
        v = qkv_ref[bb, :, 2 * M_WIDTH + h * M_HDIM:2 * M_WIDTH + (h + 1) * M_HDIM]
        ig_col = g[:, h:h + 1]
        lf_col = g[:, M_HEADS + h:M_HEADS + h + 1]
        ig_row = gt[h:h + 1, :L]
        lf_row = gt[M_HEADS + h:M_HEADS + h + 1, :L]
        b_col = jnp.sum(jnp.where(causal, lf_row, 0.0), axis=1, keepdims=True)
        b_row = jnp.sum(jnp.where(t_i <= s_i, lf_col, 0.0), axis=0, keepdims=True)
        m_prev = jnp.sum(jnp.where(lane == h, m_all, 0.0), axis=1, keepdims=True)
        inter = b_col + m_prev
        dmat = jnp.where(causal, b_col - b_row + ig_row, NEG_INF)
        m_t = jnp.maximum(inter, jnp.max(dmat, axis=1, keepdims=True))
        w_inter = jnp.exp(inter - m_t)
        w_intra = jnp.exp(dmat - m_t)
        a = w_intra * _dot_nt(q, k)
        c_h = c_ref[bb, h]
        n_h = n_ref[bb, :, sl]
        qf = q.astype(F32)
        num = w_inter * _dot_nt(q, c_h.astype(BF16)) + _dot(a.astype(BF16), v)
        den = w_inter * jnp.sum(qf * n_h, axis=1, keepdims=True) + jnp.sum(a, axis=1, keepdims=True)
        hh = num * (1.0 / jnp.maximum(jnp.abs(den), jnp.exp(-m_t)))
        mu = jnp.mean(hh, axis=1, keepdims=True)
        dv = hh - mu
        var = jnp.mean(dv * dv, axis=1, keepdims=True)
        hn = dv * lax.rsqrt(var + LN_EPS) * nw_ref[:, sl]
        h_ref[bb, :, sl] = (og_ref[bb, :, sl].astype(F32) * hn).astype(h_ref.dtype)
        m_new = m_t[L - 1:L, :]
        b_last = b_col[L - 1:L, :]
        g_state = jnp.exp(b_last + m_prev - m_new)
        g_s = jnp.exp(b_last - b_col + ig_col - m_new)
        vg = (g_s * v.astype(F32)).astype(BF16)
        c_ref[bb, h] = g_state * c_h + _dot_tn(vg, k)
        n_ref[bb, :, sl] = g_state * n_h + jnp.sum(g_s * k.astype(F32), axis=0, keepdims=True)
        m_next = jnp.where(lane == h, m_new, m_next)
    m_ref[bb] = m_next


def _mlstm(qkv, gates, og3, norm_w, c0, n0, m0):
    b, s, _ = qkv.shape
    nc = s // CHUNK
    nb = MLSTM_ROWS
    return pl.pallas_call(
        _mlstm_body,
        grid=(b // nb, nc),
        in_specs=[pl.BlockSpec((nb, CHUNK, 3 * M_WIDTH), lambda i, c: (i, c, 0)),
                  pl.BlockSpec((nb, CHUNK, LANES), lambda i, c: (i, c, 0)),
                  pl.BlockSpec((nb, CHUNK, M_WIDTH), lambda i, c: (i, c, 0)),
                  pl.BlockSpec((1, M_WIDTH), lambda i, c: (0, 0)),
                  pl.BlockSpec((nb, M_HEADS, M_HDIM, M_HDIM), lambda i, c: (i, 0, 0, 0)),
                  pl.BlockSpec((nb, 1, M_WIDTH), lambda i, c: (i, 0, 0)),
                  pl.BlockSpec((nb, 1, LANES), lambda i, c: (i, 0, 0))],
        out_specs=[pl.BlockSpec((nb, CHUNK, M_WIDTH), lambda i, c: (i, c, 0)),
                   pl.BlockSpec((nb, M_HEADS, M_HDIM, M_HDIM), lambda i, c: (i, 0, 0, 0)),
                   pl.BlockSpec((nb, 1, M_WIDTH), lambda i, c: (i, 0, 0)),
                   pl.BlockSpec((nb, 1, LANES), lambda i, c: (i, 0, 0))],
        out_shape=[jax.ShapeDtypeStruct((b, s, M_WIDTH), BF16),
                   jax.ShapeDtypeStruct((b, M_HEADS, M_HDIM, M_HDIM), F32),
                   jax.ShapeDtypeStruct((b, 1, M_WIDTH), F32),
                   jax.ShapeDtypeStruct((b, 1, LANES), F32)],
        compiler_params=_cparams(("parallel", "arbitrary")),
    )(qkv, gates, og3, norm_w, c0, n0, m0)


def _attn_body(q_ref, k_ref, v_ref, bias_ref, o_ref, *, first_valid_block):
    p = pl.program_id(1)
    row0 = pl.multiple_of(p * ATT_Q, ATT_Q)
    kw = k_ref[0, pl.ds(row0, ATT_WIN), :]
    vw = v_ref[0, pl.ds(row0, ATT_WIN), :]
    key_j = lax.broadcasted_iota(jnp.int32, (ATT_Q, ATT_WIN), 1)
    key_ok = key_j >= (first_valid_block - p) * ATT_Q
    lane = lax.broadcasted_iota(jnp.int32, (1, LANES), 1)
    for pair in range(A_HEADS // 2):
        cs = slice(pair * LANES, (pair + 1) * LANES)
        qp = q_ref[0, :, cs]
        kp = kw[:, cs]
        vp = vw[:, cs]
        o_pair = jnp.zeros((ATT_Q, LANES), F32)
        for e in range(2):
            head = 2 * pair + e
            in_head = (lane >= e * A_HDIM) & (lane < (e + 1) * A_HDIM)
            qm = jnp.where(in_head, qp, jnp.zeros_like(qp))
            s = _dot_nt(qm, kp) * (A_HDIM ** -0.5) + bias_ref[head]
            s = jnp.where(key_ok, s, NEG_INF)
            m = jnp.max(s, axis=1, keepdims=True)
            pe = jnp.exp(s - m)
            l = jnp.sum(pe, axis=1, keepdims=True)
            vm = jnp.where(in_head, vp, jnp.zeros_like(vp))
            o_pair = o_pair + _dot(pe.astype(BF16), vm) * (1.0 / l)
        o_ref[0, :, cs] = o_pair.astype(o_ref.dtype)


def _attn(q, k_pad, v_pad, bias, first_valid_block):
    b, sq, _ = q.shape
    sk = k_pad.shape[1]
    return pl.pallas_call(
        functools.partial(_attn_body, first_valid_block=first_valid_block),
        grid=(b, sq // ATT_Q),
        in_specs=[pl.BlockSpec((1, ATT_Q, A_WIDTH), lambda i, p: (i, p, 0)),
                  pl.BlockSpec((1, sk, A_WIDTH), lambda i, p: (i, 0, 0)),
                  pl.BlockSpec((1, sk, A_WIDTH), lambda i, p: (i, 0, 0)),
                  pl.BlockSpec((A_HEADS, ATT_Q, ATT_WIN), lambda i, p: (0, 0, 0))],
        out_specs=pl.BlockSpec((1, ATT_Q, A_WIDTH), lambda i, p: (i, p, 0)),
        out_shape=jax.ShapeDtypeStruct((b, sq, A_WIDTH), BF16),
        compiler_params=_cparams(("parallel", "arbitrary")),
    )(q, k_pad, v_pad, bias)


def _attn_bias(rel_bias):
    i = jnp.arange(ATT_Q)[:, None]
    j = jnp.arange(ATT_WIN)[None, :]
    n_diag = ATT_Q + ATT_WIN
    k = jnp.arange(n_diag)
    e = rel_bias.astype(F32)[:, jnp.clip(ATT_REACH + ATT_Q - 1 - k, -REL_CLIP, REL_CLIP) + REL_CLIP]
    skew = jnp.tile(e, (1, ATT_Q))[:, :ATT_Q * (n_diag - 1)].reshape(A_HEADS, ATT_Q, n_diag - 1)
    bias = skew[:, :, ATT_Q - 1:ATT_Q - 1 + ATT_WIN]
    qc = i // CHUNK
    kc = j // CHUNK - PAST_CHUNKS
    visible = (kc <= qc) & (kc >= qc - PAST_CHUNKS)
    return jnp.where(visible[None], bias, NEG_INF)


def _mix_body(hg_ref, oa_ref, gm_ref, ga_ref, x_ref, wbm_ref, wba_ref, wo_ref, g_ref, b_ref, o_ref):
    ym = _dot(hg_ref[...], wbm_ref[...])
    ya = _dot(oa_ref[...], wba_ref[...])
    gated = gm_ref[...].astype(F32) * ym + ga_ref[...].astype(F32) * ya
    mix = _dot(gated.astype(BF16), wo_ref[...])
    o_ref[...] = _layer_norm(ALPHA * x_ref[...] + mix, g_ref[...], b_ref[...])


def _mix(hg, oa, sg3, x, wbm, wba, wo, ln_g, ln_b, tb=512):
    t = x.shape[0]
    row = lambda i: (i, 0)
    const = lambda i: (0, 0)
    return pl.pallas_call(
        _mix_body,
        grid=(t // tb,),
        in_specs=[pl.BlockSpec((tb, M_WIDTH), row),
                  pl.BlockSpec((tb, A_WIDTH), row),
                  pl.BlockSpec((tb, D_MODEL), lambda i: (i, 1)),
                  pl.BlockSpec((tb, D_MODEL), lambda i: (i, 2)),
                  pl.BlockSpec((tb, D_MODEL), row),
                  pl.BlockSpec((M_WIDTH, D_MODEL), const),
                  pl.BlockSpec((A_WIDTH, D_MODEL), const),
                  pl.BlockSpec((D_MODEL, D_MODEL), const),
                  pl.BlockSpec((1, D_MODEL), const),
                  pl.BlockSpec((1, D_MODEL), const)],
        out_specs=pl.BlockSpec((tb, D_MODEL), row),
        out_shape=jax.ShapeDtypeStruct((t, D_MODEL), F32),
        compiler_params=_cparams(("parallel",)),
    )(hg, oa, sg3, sg3, x, wbm, wba, wo, ln_g, ln_b)


SUBLANES = 8
LOG2E = math.log2(math.e)


def _batcher_network(lo, hi):
    def merge(lo, hi, r):
        step = r * 2
        if step < hi - lo:
            yield from merge(lo, hi, step)
            yield from merge(lo + r, hi, step)
            yield from [(i, i + r) for i in range(lo + r, hi - r, step)]
        else:
            yield (lo, lo + r)
    if hi - lo >= 1:
        mid = lo + (hi - lo) // 2
        yield from _batcher_network(lo, mid)
        yield from _batcher_network(mid + 1, hi)
        yield from merge(lo, hi, 1)


_SORT16 = tuple(_batcher_network(0, TOPK - 1))
_BITONIC16 = tuple((i, i + s) for s in (8, 4, 2, 1) for i in range(TOPK) if (i // s) % 2 == 0)


def _compare_exchange(rows, net):
    rows = list(rows)
    for i, j in net:
        a, b = rows[i], rows[j]
        if b is None:
            continue
        if a is None:
            rows[i], rows[j] = b, None
        else:
            rows[i], rows[j] = jnp.maximum(a, b), jnp.minimum(a, b)
    return rows


def _top16_values(slabs):
    rows = list(slabs) + [None] * (TOPK - len(slabs))
    rows = _compare_exchange(rows, _SORT16)
    for shift in (4, 2, 1):
        merged = []
        for r in range(TOPK):
            a, b = rows[r], rows[TOPK - 1 - r]
            b = None if b is None else pltpu.roll(b, shift, axis=0)
            merged.append(b if a is None else a if b is None else jnp.maximum(a, b))
        rows = _compare_exchange(merged, _BITONIC16)
    return rows


def _by_sublane(rows, sub):
    out = rows[SUBLANES - 1]
    for g in range(SUBLANES - 2, -1, -1):
        out = jnp.where(sub == g, rows[g], out)
    return out


def _peer_sel_body(x_ref, wq_ref, sk_ref, s_ref, aux_ref):
    tb = x_ref.shape[0]
    qt = _dot_nt(wq_ref[...], x_ref[...].astype(BF16))
    sub = lax.broadcasted_iota(jnp.int32, (SUBLANES, tb), 0)
    for h in range(R_HEADS):
        top = []
        for p in range(2):
            r = h * 2 + p
            qhp = qt[r * HALF_KEY:(r + 1) * HALF_KEY, :].astype(BF16)
            st = _dot(sk_ref[r], qhp) * LOG2E
            s_ref[r] = st
            top.append(_top16_values([st[g * SUBLANES:(g + 1) * SUBLANES, :] for g in range(N_KEYS // SUBLANES)]))
        c1, c2 = top
        v1_lo, v1_hi = _by_sublane(c1[:SUBLANES], sub), _by_sublane(c1[SUBLANES:], sub)
        v2_hi = _by_sublane(c2[SUBLANES:], sub)
        firsts = [v1_lo] * SUBLANES + [c1[0], v1_hi]
        seconds = c2[:SUBLANES] + [v2_hi, c2[0]]
        cand = [f + s for f, s in zip(firsts, seconds)]
        cv = _top16_values(cand)
        tau, best = cv[TOPK - 1], cv[0]
        picked = [c >= tau for c in cand]
        z = sum(jnp.where(pk, jnp.exp2(c - best), 0.0) for pk, c in zip(picked, cand))
        shift = best[0:1, :] + jnp.log2(jnp.sum(z, axis=0, keepdims=True))
        shifted = [jnp.where(pk, (f - shift) + s, jnp.inf) for pk, f, s in zip(picked, firsts, seconds)]
        tau_shifted = functools.reduce(jnp.minimum, shifted)
        s_ref[2 * h] = s_ref[2 * h] - shift
        aux_ref[h:h + 1, :] = jnp.min(tau_shifted, axis=0, keepdims=True)


def _peer_sel(x, wq_t, subkeys, tb=256):
    t = x.shape[0]
    return pl.pallas_call(
        _peer_sel_body,
        grid=(t // tb,),
        in_specs=[pl.BlockSpec((tb, D_MODEL), lambda i: (i, 0)),
                  pl.BlockSpec((2 * R_HEADS * HALF_KEY, D_MODEL), lambda i: (0, 0)),
                  pl.BlockSpec((2 * R_HEADS, N_KEYS, HALF_KEY), lambda i: (0, 0, 0))],
        out_specs=[pl.BlockSpec((2 * R_HEADS, N_KEYS, tb), lambda i: (0, 0, i)),
                   pl.BlockSpec((R_HEADS, tb), lambda i: (0, i))],
        out_shape=[jax.ShapeDtypeStruct((2 * R_HEADS, N_KEYS, t), F32),
                   jax.ShapeDtypeStruct((R_HEADS, t), F32)],
        compiler_params=_cparams(("parallel",)),
    )(x, wq_t, subkeys)


def _gelu_tanh(x):
    return 0.5 * x * (1.0 + jnp.tanh(math.sqrt(2.0 / math.pi) * (x + 0.044715 * (x * x * x))))


PEER_TILE = SUBLANES * N_KEYS
PEER_TILES = N_EXPERTS // PEER_TILE


def _peer_ffn_body(x_ref, u_ref, vt_ref, s_ref, aux_ref, pe_ref, plp_ref, plg_ref, g_ref, b_ref,
                   o_ref, xt_scr, acc_scr, p_scr):
    e = pl.program_id(1)
    tb = x_ref.shape[0]

    @pl.when(e == 0)
    def _():
        xt_scr[...] = x_ref[...].T.astype(BF16)
        acc_scr[...] = jnp.zeros_like(acc_scr)

    act = _gelu_tanh(_dot(u_ref[...], xt_scr[...]).astype(BF16))
    i0 = pl.multiple_of(e * SUBLANES, SUBLANES)
    for ii in range(SUBLANES):
        rs = slice(ii * N_KEYS, (ii + 1) * N_KEYS)
        for lg in range(tb // LANES):
            ls = slice(lg * LANES, (lg + 1) * LANES)
            w = jnp.zeros((N_KEYS, LANES), F32)
            for h in range(R_HEADS):
                s1 = s_ref[2 * h, pl.ds(i0, SUBLANES), ls][ii:ii + 1, :]
                s = s1 + s_ref[2 * h + 1, :, ls]
                w = w + jnp.where(s >= aux_ref[h:h + 1, ls], jnp.exp2(s), 0.0)
            p_scr[rs, ls] = act[rs, ls] * w.astype(BF16)
    acc_scr[...] += _dot(vt_ref[0], p_scr[...])

    @pl.when(e == pl.num_programs(1) - 1)
    def _():
        x = x_ref[...]
        y = _layer_norm(ALPHA * x + acc_scr[...].T, g_ref[...], b_ref[...])
        ple = _dot(pe_ref[...].astype(BF16), plp_ref[...])
        gate = _sigmoid(_dot(y.astype(BF16), plg_ref[...]))
        o_ref[...] = y + ple * gate


def _peer_ffn(x, u, vt_tiles, scores, aux, pe, plp, plg, ln_g, ln_b, tb=512):
    t = x.shape[0]
    row = lambda i, e: (i, 0)
    const = lambda i, e: (0, 0)
    return pl.pallas_call(
        _peer_ffn_body,
        grid=(t // tb, PEER_TILES),
        in_specs=[pl.BlockSpec((tb, D_MODEL), row),
                  pl.BlockSpec((PEER_TILE, D_MODEL), lambda i, e: (e, 0)),
                  pl.BlockSpec((1, D_MODEL, PEER_TILE), lambda i, e: (e, 0, 0)),
                  pl.BlockSpec((2 * R_HEADS, N_KEYS, tb), lambda i, e: (0, 0, i)),
                  pl.BlockSpec((R_HEADS, tb), lambda i, e: (0, i)),
                  pl.BlockSpec((tb, PLE_DIM), row),
                  pl.BlockSpec((PLE_DIM, D_MODEL), const),
                  pl.BlockSpec((D_MODEL, D_MODEL), const),
                  pl.BlockSpec((1, D_MODEL), const),
                  pl.BlockSpec((1, D_MODEL), const)],
        out_specs=pl.BlockSpec((tb, D_MODEL), row),
        out_shape=jax.ShapeDtypeStruct((t, D_MODEL), F32),
        scratch_shapes=[pltpu.VMEM((D_MODEL, tb), BF16),
                        pltpu.VMEM((D_MODEL, tb), F32),
                        pltpu.VMEM((PEER_TILE, tb), BF16)],
        compiler_params=_cparams(("parallel", "arbitrary")),
    )(x, u, vt_tiles, scores, aux, pe, plp, plg, ln_g, ln_b)


def _prep_layer(w_in, if_bias, norm_w, rel_bias, w_bm, w_ba, w_o, ln1g, ln1b, ln2g, ln2b,
                pq, psk, pu, pv, plp, plg):
    c = 0
    cols = {}
    for name, width in (("mq", M_WIDTH), ("mk", M_WIDTH), ("mv", M_WIDTH), ("mi", M_HEADS), ("mf", M_HEADS),
                        ("mo", M_WIDTH), ("aq", A_WIDTH), ("ak", A_WIDTH), ("av", A_WIDTH),
                        ("gm", D_MODEL), ("ga", D_MODEL)):
        cols[name] = w_in[:, c:c + width]
        c += width
    cat = lambda names: jnp.concatenate([cols[n] for n in names], axis=1)
    ones = lambda n: jnp.ones((1, n), F32)
    w_if = jnp.pad(cat(("mi", "mf")), ((0, 0), (0, LANES - 2 * M_HEADS)))
    b_if = jnp.pad(if_bias.reshape(1, 2 * M_HEADS).astype(F32), ((0, 0), (0, LANES - 2 * M_HEADS)))
    return dict(
        w_mqkv=cat(("mq", "mk", "mv")).astype(BF16),
        s_mqkv=jnp.concatenate([ones(M_WIDTH), ones(M_WIDTH) * (M_HDIM ** -0.5), ones(M_WIDTH)], axis=1),
        w_sig=cat(("mo", "gm", "ga")).astype(BF16), s_sig=ones(3 * D_MODEL),
        w_aqkv=cat(("aq", "ak", "av")).astype(BF16), s_aqkv=ones(3 * A_WIDTH),
        w_if=w_if, b_if=b_if,
        norm_w=norm_w.reshape(1, M_WIDTH).astype(F32),
        bias=_attn_bias(rel_bias),
        w_bm=w_bm.astype(BF16), w_ba=w_ba.astype(BF16), w_o=w_o.astype(BF16),
        ln1g=ln1g.reshape(1, D_MODEL), ln1b=ln1b.reshape(1, D_MODEL),
        ln2g=ln2g.reshape(1, D_MODEL), ln2b=ln2b.reshape(1, D_MODEL),
        wq_t=pq.T.astype(BF16),
        subkeys=psk.reshape(2 * R_HEADS, N_KEYS, HALF_KEY).astype(BF16),
        u=pu.astype(BF16),
        vt=pv.reshape(PEER_TILES, PEER_TILE, D_MODEL).transpose(0, 2, 1).astype(BF16),
        plp=plp.astype(BF16), plg=plg.astype(BF16),
    )


def _layer(x3, pe3, lw, state, cache):
    b, s, _ = x3.shape
    t = b * s
    x = x3.reshape(t, D_MODEL)
    mqkv, = _proj(x, lw["w_mqkv"], lw["s_mqkv"], None, (BF16,), tb=1024, nb=1024)
    sig3, = _proj(x, lw["w_sig"], lw["s_sig"], "sigmoid", (BF16,), tb=1024, nb=1024)
    aqkv, akv32 = _proj(x, lw["w_aqkv"], lw["s_aqkv"], None, (BF16, F32), tb=1024, nb=512)
    gates = _gates(x, lw["w_if"], lw["b_if"])

    if state is None:
        c0 = jnp.zeros((b, M_HEADS, M_HDIM, M_HDIM), F32)
        n0 = jnp.zeros((b, 1, M_WIDTH), F32)
        m0 = jnp.zeros((b, 1, LANES), F32)
    else:
        c0 = state[0].astype(F32)
        n0 = state[1].astype(F32).reshape(b, 1, M_WIDTH)
        m0 = jnp.pad(state[2].astype(F32), ((0, 0), (0, LANES - M_HEADS))).reshape(b, 1, LANES)
    hg, c_new, n_new, m_new = _mlstm(mqkv.reshape(b, s, 3 * M_WIDTH), gates.reshape(b, s, LANES),
                                     sig3.reshape(b, s, 3 * D_MODEL), lw["norm_w"], c0, n0, m0)
    n_new = n_new.reshape(b, M_HEADS, M_HDIM)
    m_new = m_new.reshape(b, LANES)[:, :M_HEADS]

    a3 = aqkv.reshape(b, s, 3 * A_WIDTH)
    q, k, v = a3[..., :A_WIDTH], a3[..., A_WIDTH:2 * A_WIDTH], a3[..., 2 * A_WIDTH:]
    kv32 = akv32.reshape(b, s, 3 * A_WIDTH)
    if cache is None:
        hist_k = jnp.zeros((b, ATT_REACH, A_WIDTH), BF16)
        hist_v = hist_k
        first_valid_block = ATT_REACH // ATT_Q
        keep = min(ATT_REACH, s)
        k_out = kv32[:, s - keep:, A_WIDTH:2 * A_WIDTH].reshape(b, keep, A_HEADS, A_HDIM)
        v_out = kv32[:, s - keep:, 2 * A_WIDTH:].reshape(b, keep, A_HEADS, A_HDIM)
    else:
        hist_k = cache[0].reshape(b, ATT_REACH, A_WIDTH).astype(BF16)
        hist_v = cache[1].reshape(b, ATT_REACH, A_WIDTH).astype(BF16)
        first_valid_block = 0
        k_out = kv32[..., A_WIDTH:2 * A_WIDTH].reshape(b, s, A_HEADS, A_HDIM)
        v_out = kv32[..., 2 * A_WIDTH:].reshape(b, s, A_HEADS, A_HDIM)
    sq = -(-s // ATT_Q) * ATT_Q
    tail = ((0, 0), (0, sq - s), (0, 0))
    oa = _attn(jnp.pad(q, tail), jnp.pad(jnp.concatenate([hist_k, k], axis=1), tail),
               jnp.pad(jnp.concatenate([hist_v, v], axis=1), tail), lw["bias"], first_valid_block)
    oa = oa[:, :s].reshape(t, A_WIDTH)

    x1 = _mix(hg.reshape(t, M_WIDTH), oa, sig3, x, lw["w_bm"], lw["w_ba"], lw["w_o"], lw["ln1g"], lw["ln1b"])
    scores, aux = _peer_sel(x1, lw["wq_t"], lw["subkeys"])
    x3_out = _peer_ffn(x1, lw["u"], lw["vt"], scores, aux, pe3.reshape(t, PLE_DIM), lw["plp"], lw["plg"],
                       lw["ln2g"], lw["ln2b"])
    return x3_out.reshape(b, s, D_MODEL), (c_new, n_new, m_new), (k_out, v_out)


def kernel(x_prompt, x_sample, cache_attn_k, cache_attn_v, state_mlstm_C, state_mlstm_n, state_mlstm_m,
           p_prompt, p_sample, w_in, mlstm_if_bias, mlstm_norm_w, attn_rel_bias, w_branch_m, w_branch_a,
           w_out, ln1_g, ln1_b, ln2_g, ln2_b, peer_wq, peer_subkeys, peer_u, peer_v, ple_proj, ple_gate):
    layer_w = (w_in, mlstm_if_bias, mlstm_norm_w, attn_rel_bias, w_branch_m, w_branch_a, w_out,
               ln1_g, ln1_b, ln2_g, ln2_b, peer_wq, peer_subkeys, peer_u, peer_v, ple_proj, ple_gate)
    yp, ys = x_prompt, x_sample
    outs_p = [[] for _ in range(5)]
    outs_s = [[] for _ in range(5)]
    for i in range(DEPTH):
        lw = _prep_layer(*(w[i] for w in layer_w))
        yp, (cp, np_, mp), (kp, vp) = _layer(yp, p_prompt[i], lw, None, None)
        ys, (cs, ns, ms), (ks, vs) = _layer(
            ys, p_sample[i], lw, (state_mlstm_C[i], state_mlstm_n[i], state_mlstm_m[i]),
            (cache_attn_k[i], cache_attn_v[i]))
        for lst, val in zip(outs_p, (kp, vp, cp, np_, mp)):
            lst.append(val)
        for lst, val in zip(outs_s, (ks, vs, cs, ns, ms)):
            lst.append(val)
    st = lambda l: jnp.stack(l, axis=0)
    return (yp, ys) + tuple(st(l) for l in outs_p) + tuple(st(l) for l in outs_s)
```

```python
import functools
import math

import jax
import jax.numpy as jnp
from jax import lax
from jax.experimental import pallas as pl
from jax.experimental.pallas import tpu as pltpu

F32 = jnp.float32
BF16 = jnp.bfloat16

D_MODEL = 1024
CHUNK = 64
M_HEADS = 4
M_HDIM = 256
M_WIDTH = 1024
A_HEADS = 8
A_HDIM = 64
A_WIDTH = 512
PAST_CHUNKS = 8
ATT_REACH = 512
REL_CLIP = 128
R_HEADS = 8
N_KEYS = 128
N_EXPERTS = N_KEYS * N_KEYS
HALF_KEY = 128
TOPK = 16
PLE_DIM = 256
DEPTH = 2
ALPHA = (2 * DEPTH) ** 0.25
LN_EPS = 1e-5

LANES = 128
MLSTM_ROWS = 2
ATT_Q = 2 * CHUNK
ATT_WIN = ATT_REACH + ATT_Q
VMEM_LIMIT = 48 * 1024 * 1024
NEG_INF = float("-inf")


def _cparams(sem):
    return pltpu.CompilerParams(dimension_semantics=sem, vmem_limit_bytes=VMEM_LIMIT)


def _sigmoid(x):
    return 1.0 / (1.0 + jnp.exp(-x))


def _dot(a, b):
    return jnp.dot(a, b, preferred_element_type=F32)


def _dot_nt(a, b):
    return lax.dot_general(a, b, (((1,), (1,)), ((), ())), preferred_element_type=F32)


def _dot_tn(a, b):
    return lax.dot_general(a, b, (((0,), (0,)), ((), ())), preferred_element_type=F32)


def _layer_norm(y, g, b):
    mu = jnp.mean(y, axis=-1, keepdims=True)
    d = y - mu
    var = jnp.mean(d * d, axis=-1, keepdims=True)
    return d * lax.rsqrt(var + LN_EPS) * g + b


def _proj_body(x_ref, w_ref, s_ref, *o_refs, act):
    acc = _dot(x_ref[...].astype(BF16), w_ref[...]) * s_ref[...]
    if act == "sigmoid":
        acc = _sigmoid(acc)
    for o in o_refs:
        o[...] = acc.astype(o.dtype)


def _proj(x, w, scale, act, out_dtypes, tb=512, nb=512):
    t, k = x.shape
    n = w.shape[1]
    return pl.pallas_call(
        functools.partial(_proj_body, act=act),
        grid=(t // tb, n // nb),
        in_specs=[pl.BlockSpec((tb, k), lambda i, j: (i, 0)),
                  pl.BlockSpec((k, nb), lambda i, j: (0, j)),
                  pl.BlockSpec((1, nb), lambda i, j: (0, j))],
        out_specs=[pl.BlockSpec((tb, nb), lambda i, j: (i, j)) for _ in out_dtypes],
        out_shape=[jax.ShapeDtypeStruct((t, n), d) for d in out_dtypes],
        compiler_params=_cparams(("parallel", "arbitrary")),
    )(x, w, scale)


def _split3(v):
    hi = v.astype(BF16)
    r = v - hi.astype(F32)
    mid = r.astype(BF16)
    lo = (r - mid.astype(F32)).astype(BF16)
    return hi, mid, lo


def _gate_body(x_ref, w_ref, b_ref, o_ref):
    xh, xm, xl = _split3(x_ref[...])
    wh, wm, wl = _split3(w_ref[...])
    z = (_dot(xh, wh) + (_dot(xh, wm) + _dot(xm, wh))
         + (_dot(xh, wl) + _dot(xm, wm) + _dot(xl, wh))) + b_ref[...]
    col = lax.broadcasted_iota(jnp.int32, z.shape, 1)
    log_sig = jnp.minimum(z, 0.0) - jnp.log1p(jnp.exp(-jnp.abs(z)))
    o_ref[...] = jnp.where(col < M_HEADS, z, log_sig)


def _gates(x, w_if, b_if, tb=512):
    t, k = x.shape
    return pl.pallas_call(
        _gate_body,
        grid=(t // tb,),
        in_specs=[pl.BlockSpec((tb, k), lambda i: (i, 0)),
                  pl.BlockSpec((k, LANES), lambda i: (0, 0)),
                  pl.BlockSpec((1, LANES), lambda i: (0, 0))],
        out_specs=pl.BlockSpec((tb, LANES), lambda i: (i, 0)),
        out_shape=jax.ShapeDtypeStruct((t, LANES), F32),
        compiler_params=_cparams(("parallel",)),
    )(x, w_if, b_if)


def _mlstm_body(qkv_ref, g_ref, og_ref, nw_ref, c0_ref, n0_ref, m0_ref,
                h_ref, c_ref, n_ref, m_ref):
    @pl.when(pl.program_id(1) == 0)
    def _():
        c_ref[...] = c0_ref[...]
        n_ref[...] = n0_ref[...]
        m_ref[...] = m0_ref[...]

    for bb in range(qkv_ref.shape[0]):
        _mlstm_chunk_update(bb, qkv_ref, g_ref, og_ref, nw_ref, h_ref, c_ref, n_ref, m_ref)


def _mlstm_chunk_update(bb, qkv_ref, g_ref, og_ref, nw_ref, h_ref, c_ref, n_ref, m_ref):
    L = CHUNK
    g = g_ref[bb]
    gt = jnp.concatenate([g, jnp.zeros((LANES - L, LANES), F32)], axis=0).T
    t_i = lax.broadcasted_iota(jnp.int32, (L, L), 0)
    s_i = lax.broadcasted_iota(jnp.int32, (L, L), 1)
    causal = s_i <= t_i
    lane = lax.broadcasted_iota(jnp.int32, (1, LANES), 1)
    m_all = m_ref[bb]
    m_next = m_all
    for h in range(M_HEADS):
        sl = slice(h * M_HDIM, (h + 1) * M_HDIM)
        q = qkv_ref[bb, :, h * M_HDIM:(h + 1) * M_HDIM]
        k = qkv_ref[bb, :, M_WIDTH + h * M_HDIM:M_WIDTH + (h + 1) * M_HDIM]
        v = qkv_ref[bb, :, 2 * M_WIDTH + h * M_HDIM:2 * M_WIDTH + (h + 1) * M_HDIM]
        ig_col = g[:, h:h + 1]
        lf_col = g[:, M_HEADS + h:M_HEADS + h + 1]
        ig_row = gt[h:h + 1, :L]
        lf_row = gt[M_HEADS + h:M_HEADS + h + 1, :L]
        b_col = jnp.sum(jnp.where(causal, lf_row, 0.0), axis=1, keepdims=True)
        b_row = jnp.sum(jnp.where(t_i <= s_i, lf_col, 0.0), axis=0, keepdims=True)
        m_prev = jnp.sum(jnp.where(lane == h, m_all, 0.0), axis=1, keepdims=True)
        inter = b_col + m_prev
        dmat = jnp.where(causal, b_col - b_row + ig_row, NEG_INF)
        m_t = jnp.maximum(inter, jnp.max(dmat, axis=1, keepdims=True))
        w_inter = jnp.exp(inter - m_t)
        w_intra = jnp.exp(dmat - m_t)
        a = w_intra * _dot_nt(q, k)
        c_h = c_ref[bb, h]
        n_h = n_ref[bb, :, sl]
        qf = q.astype(F32)
        num = w_inter * _dot_nt(q, c_h.astype(BF16)) + _dot(a.astype(BF16), v)
        den = w_inter * jnp.sum(qf * n_h, axis=1, keepdims=True) + jnp.sum(a, axis=1, keepdims=True)
        hh = num * (1.0 / jnp.maximum(jnp.abs(den), jnp.exp(-m_t)))
        mu = jnp.mean(hh, axis=1, keepdims=True)
        dv = hh - mu
        var = jnp.mean(dv * dv, axis=1, keepdims=True)
        hn = dv * lax.rsqrt(var + LN_EPS) * nw_ref[:, sl]
        h_ref[bb, :, sl] = (og_ref[bb, :, sl].astype(F32) * hn).astype(h_ref.dtype)
        m_new = m_t[L - 1:L, :]
        b_last = b_col[L - 1:L, :]
        g_state = jnp.exp(b_last + m_prev - m_new)
        g_s = jnp.exp(b_last - b_col + ig_col - m_new)
        vg = (g_s * v.astype(F32)).astype(BF16)
        c_ref[bb, h] = g_state * c_h + _dot_tn(vg, k)
        n_ref[bb, :, sl] = g_state * n_h + jnp.sum(g_s * k.astype(F32), axis=0, keepdims=True)
        m_next = jnp.where(lane == h, m_new, m_next)
    m_ref[bb] = m_next


def _mlstm(qkv, gates, og3, norm_w, c0, n0, m0):
    b, s, _ = qkv.shape
    nc = s // CHUNK
    nb = MLSTM_ROWS
    return pl.pallas_call(
        _mlstm_body,
        grid=(b // nb, nc),
        in_specs=[pl.BlockSpec((nb, CHUNK, 3 * M_WIDTH), lambda i, c: (i, c, 0)),
                  pl.BlockSpec((nb, CHUNK, LANES), lambda i, c: (i, c, 0)),
                  pl.BlockSpec((nb, CHUNK, M_WIDTH), lambda i, c: (i, c, 0)),
                  pl.BlockSpec((1, M_WIDTH), lambda i, c: (0, 0)),
                  pl.BlockSpec((nb, M_HEADS, M_HDIM, M_HDIM), lambda i, c: (i, 0, 0, 0)),
                  pl.BlockSpec((nb, 1, M_WIDTH), lambda i, c: (i, 0, 0)),
                  pl.BlockSpec((nb, 1, LANES), lambda i, c: (i, 0, 0))],
        out_specs=[pl.BlockSpec((nb, CHUNK, M_WIDTH), lambda i, c: (i, c, 0)),
                   pl.BlockSpec((nb, M_HEADS, M_HDIM, M_HDIM), lambda i, c: (i, 0, 0, 0)),
                   pl.BlockSpec((nb, 1, M_WIDTH), lambda i, c: (i, 0, 0)),
                   pl.BlockSpec((nb, 1, LANES), lambda i, c: (i, 0, 0))],
        out_shape=[jax.ShapeDtypeStruct((b, s, M_WIDTH), BF16),
                   jax.ShapeDtypeStruct((b, M_HEADS, M_HDIM, M_HDIM), F32),
                   jax.ShapeDtypeStruct((b, 1, M_WIDTH), F32),
                   jax.ShapeDtypeStruct((b, 1, LANES), F32)],
        compiler_params=_cparams(("parallel", "arbitrary")),
    )(qkv, gates, og3, norm_w, c0, n0, m0)


def _attn_body(q_ref, k_ref, v_ref, bias_ref, o_ref, *, first_valid_block):
    p = pl.program_id(1)
    row0 = pl.multiple_of(p * ATT_Q, ATT_Q)
    kw = k_ref[0, pl.ds(row0, ATT_WIN), :]
    vw = v_ref[0, pl.ds(row0, ATT_WIN), :]
    key_j = lax.broadcasted_iota(jnp.int32, (ATT_Q, ATT_WIN), 1)
    key_ok = key_j >= (first_valid_block - p) * ATT_Q
    lane = lax.broadcasted_iota(jnp.int32, (1, LANES), 1)
    for pair in range(A_HEADS // 2):
        cs = slice(pair * LANES, (pair + 1) * LANES)
        qp = q_ref[0, :, cs]
        kp = kw[:, cs]
        vp = vw[:, cs]
        o_pair = jnp.zeros((ATT_Q, LANES), F32)
        for e in range(2):
            head = 2 * pair + e
            in_head = (lane >= e * A_HDIM) & (lane < (e + 1) * A_HDIM)
            qm = jnp.where(in_head, qp, jnp.zeros_like(qp))
            s = _dot_nt(qm, kp) * (A_HDIM ** -0.5) + bias_ref[head]
            s = jnp.where(key_ok, s, NEG_INF)
            m = jnp.max(s, axis=1, keepdims=True)
            pe = jnp.exp(s - m)
            l = jnp.sum(pe, axis=1, keepdims=True)
            vm = jnp.where(in_head, vp, jnp.zeros_like(vp))
            o_pair = o_pair + _dot(pe.astype(BF16), vm) * (1.0 / l)
        o_ref[0, :, cs] = o_pair.astype(o_ref.dtype)


def _attn(q, k_pad, v_pad, bias, first_valid_block):
    b, sq, _ = q.shape
    sk = k_pad.shape[1]
    return pl.pallas_call(
        functools.partial(_attn_body, first_valid_block=first_valid_block),
        grid=(b, sq // ATT_Q),
        in_specs=[pl.BlockSpec((1, ATT_Q, A_WIDTH), lambda i, p: (i, p, 0)),
                  pl.BlockSpec((1, sk, A_WIDTH), lambda i, p: (i, 0, 0)),
                  pl.BlockSpec((1, sk, A_WIDTH), lambda i, p: (i, 0, 0)),
                  pl.BlockSpec((A_HEADS, ATT_Q, ATT_WIN), lambda i, p: (0, 0, 0))],
        out_specs=pl.BlockSpec((1, ATT_Q, A_WIDTH), lambda i, p: (i, p, 0)),
        out_shape=jax.ShapeDtypeStruct((b, sq, A_WIDTH), BF16),
        compiler_params=_cparams(("parallel", "arbitrary")),
    )(q, k_pad, v_pad, bias)


def _attn_bias(rel_bias):
    i = jnp.arange(ATT_Q)[:, None]
    j = jnp.arange(ATT_WIN)[None, :]
    n_diag = ATT_Q + ATT_WIN
    k = jnp.arange(n_diag)
    e = rel_bias.astype(F32)[:, jnp.clip(ATT_REACH + ATT_Q - 1 - k, -REL_CLIP, REL_CLIP) + REL_CLIP]
    skew = jnp.tile(e, (1, ATT_Q))[:, :ATT_Q * (n_diag - 1)].reshape(A_HEADS, ATT_Q, n_diag - 1)
    bias = skew[:, :, ATT_Q - 1:ATT_Q - 1 + ATT_WIN]
    qc = i // CHUNK
    kc = j // CHUNK - PAST_CHUNKS
    visible = (kc <= qc) & (kc >= qc - PAST_CHUNKS)
    return jnp.where(visible[None], bias, NEG_INF)


def _mix_body(hg_ref, oa_ref, gm_ref, ga_ref, x_ref, wbm_ref, wba_ref, wo_ref, g_ref, b_ref, o_ref):
    ym = _dot(hg_ref[...], wbm_ref[...])
    ya = _dot(oa_ref[...], wba_ref[...])
    gated = gm_ref[...].astype(F32) * ym + ga_ref[...].astype(F32) * ya
    mix = _dot(gated.astype(BF16), wo_ref[...])
    o_ref[...] = _layer_norm(ALPHA * x_ref[...] + mix, g_ref[...], b_ref[...])


def _mix(hg, oa, sg3, x, wbm, wba, wo, ln_g, ln_b, tb=512):
    t = x.shape[0]
    row = lambda i: (i, 0)
    const = lambda i: (0, 0)
    return pl.pallas_call(
        _mix_body,
        grid=(t // tb,),
        in_specs=[pl.BlockSpec((tb, M_WIDTH), row),
                  pl.BlockSpec((tb, A_WIDTH), row),
                  pl.BlockSpec((tb, D_MODEL), lambda i: (i, 1)),
                  pl.BlockSpec((tb, D_MODEL), lambda i: (i, 2)),
                  pl.BlockSpec((tb, D_MODEL), row),
                  pl.BlockSpec((M_WIDTH, D_MODEL), const),
                  pl.BlockSpec((A_WIDTH, D_MODEL), const),
                  pl.BlockSpec((D_MODEL, D_MODEL), const),
                  pl.BlockSpec((1, D_MODEL), const),
                  pl.BlockSpec((1, D_MODEL), const)],
        out_specs=pl.BlockSpec((tb, D_MODEL), row),
        out_shape=jax.ShapeDtypeStruct((t, D_MODEL), F32),
        compiler_params=_cparams(("parallel",)),
    )(hg, oa, sg3, sg3, x, wbm, wba, wo, ln_g, ln_b)


SUBLANES = 8
LOG2E = math.log2(math.e)


def _batcher_network(lo, hi):
    def merge(lo, hi, r):
        step = r * 2
        if step < hi - lo:
            yield from merge(lo, hi, step)
            yield from merge(lo + r, hi, step)
            yield from [(i, i + r) for i in range(lo + r, hi - r, step)]
        else:
            yield (lo, lo + r)
    if hi - lo >= 1:
        mid = lo + (hi - lo) // 2
        yield from _batcher_network(lo, mid)
        yield from _batcher_network(mid + 1, hi)
        yield from merge(lo, hi, 1)


_SORT16 = tuple(_batcher_network(0, TOPK - 1))
_BITONIC16 = tuple((i, i + s) for s in (8, 4, 2, 1) for i in range(TOPK) if (i // s) % 2 == 0)


def _compare_exchange(rows, net):
    rows = list(rows)
    for i, j in net:
        a, b = rows[i], rows[j]
        if b is None:
            continue
        if a is None:
            rows[i], rows[j] = b, None
        else:
            rows[i], rows[j] = jnp.maximum(a, b), jnp.minimum(a, b)
    return rows


def _top16_values(slabs):
    rows = list(slabs) + [None] * (TOPK - len(slabs))
    rows = _compare_exchange(rows, _SORT16)
    for shift in (4, 2, 1):
        merged = []
        for r in range(TOPK):
            a, b = rows[r], rows[TOPK - 1 - r]
            b = None if b is None else pltpu.roll(b, shift, axis=0)
            merged.append(b if a is None else a if b is None else jnp.maximum(a, b))
        rows = _compare_exchange(merged, _BITONIC16)
    return rows


def _by_sublane(rows, sub):
    out = rows[SUBLANES - 1]
    for g in range(SUBLANES - 2, -1, -1):
        out = jnp.where(sub == g, rows[g], out)
    return out


def _peer_sel_body(x_ref, wq_ref, sk_ref, s_ref, aux_ref):
    tb = x_ref.shape[0]
    qt = _dot_nt(wq_ref[...], x_ref[...].astype(BF16))
    sub = lax.broadcasted_iota(jnp.int32, (SUBLANES, tb), 0)
    for h in range(R_HEADS):
        top = []
        for p in range(2):
            r = h * 2 + p
            qhp = qt[r * HALF_KEY:(r + 1) * HALF_KEY, :].astype(BF16)
            st = _dot(sk_ref[r], qhp) * LOG2E
            s_ref[r] = st
            top.append(_top16_values([st[g * SUBLANES:(g + 1) * SUBLANES, :] for g in range(N_KEYS // SUBLANES)]))
        c1, c2 = top
        v1_lo, v1_hi = _by_sublane(c1[:SUBLANES], sub), _by_sublane(c1[SUBLANES:], sub)
        v2_hi = _by_sublane(c2[SUBLANES:], sub)
        firsts = [v1_lo] * SUBLANES + [c1[0], v1_hi]
        seconds = c2[:SUBLANES] + [v2_hi, c2[0]]
        cand = [f + s for f, s in zip(firsts, seconds)]
        cv = _top16_values(cand)
        tau, best = cv[TOPK - 1], cv[0]
        picked = [c >= tau for c in cand]
        z = sum(jnp.where(pk, jnp.exp2(c - best), 0.0) for pk, c in zip(picked, cand))
        shift = best[0:1, :] + jnp.log2(jnp.sum(z, axis=0, keepdims=True))
        shifted = [jnp.where(pk, (f - shift) + s, jnp.inf) for pk, f, s in zip(picked, firsts, seconds)]
        tau_shifted = functools.reduce(jnp.minimum, shifted)
        s_ref[2 * h] = s_ref[2 * h] - shift
        aux_ref[h:h + 1, :] = jnp.min(tau_shifted, axis=0, keepdims=True)


def _peer_sel(x, wq_t, subkeys, tb=256):
    t = x.shape[0]
    return pl.pallas_call(
        _peer_sel_body,
        grid=(t // tb,),
        in_specs=[pl.BlockSpec((tb, D_MODEL), lambda i: (i, 0)),
                  pl.BlockSpec((2 * R_HEADS * HALF_KEY, D_MODEL), lambda i: (0, 0)),
                  pl.BlockSpec((2 * R_HEADS, N_KEYS, HALF_KEY), lambda i: (0, 0, 0))],
        out_specs=[pl.BlockSpec((2 * R_HEADS, N_KEYS, tb), lambda i: (0, 0, i)),
                   pl.BlockSpec((R_HEADS, tb), lambda i: (0, i))],
        out_shape=[jax.ShapeDtypeStruct((2 * R_HEADS, N_KEYS, t), F32),
                   jax.ShapeDtypeStruct((R_HEADS, t), F32)],
        compiler_params=_cparams(("parallel",)),
    )(x, wq_t, subkeys)


def _gelu_tanh(x):
    return 0.5 * x * (1.0 + jnp.tanh(math.sqrt(2.0 / math.pi) * (x + 0.044715 * (x * x * x))))


PEER_TILE_KEYS = 2 * SUBLANES
PEER_TILE = PEER_TILE_KEYS * N_KEYS
PEER_TILES = N_EXPERTS // PEER_TILE


def _peer_ffn_body(x_ref, u_ref, vt_ref, s_ref, aux_ref, pe_ref, plp_ref, plg_ref, g_ref, b_ref,
                   o_ref, xt_scr, acc_scr, p_scr):
    e = pl.program_id(1)
    tb = x_ref.shape[0]

    @pl.when(e == 0)
    def _():
        xt_scr[...] = x_ref[...].T.astype(BF16)
        acc_scr[...] = jnp.zeros_like(acc_scr)

    act = _gelu_tanh(_dot(u_ref[...], xt_scr[...]).astype(BF16))
    i0 = pl.multiple_of(e * PEER_TILE_KEYS, SUBLANES)
    for ii in range(PEER_TILE_KEYS):
        grp, sub = divmod(ii, SUBLANES)
        rs = slice(ii * N_KEYS, (ii + 1) * N_KEYS)
        for lg in range(tb // LANES):
            ls = slice(lg * LANES, (lg + 1) * LANES)
            w = jnp.zeros((N_KEYS, LANES), F32)
            for h in range(R_HEADS):
                s1 = s_ref[2 * h, pl.ds(i0 + grp * SUBLANES, SUBLANES), ls][sub:sub + 1, :]
                s = s1 + s_ref[2 * h + 1, :, ls]
                w = w + jnp.where(s >= aux_ref[h:h + 1, ls], jnp.exp2(s), 0.0)
            p_scr[rs, ls] = act[rs, ls] * w.astype(BF16)
    acc_scr[...] += _dot(vt_ref[0], p_scr[...])

    @pl.when(e == pl.num_programs(1) - 1)
    def _():
        x = x_ref[...]
        y = _layer_norm(ALPHA * x + acc_scr[...].T, g_ref[...], b_ref[...])
        ple = _dot(pe_ref[...].astype(BF16), plp_ref[...])
        gate = _sigmoid(_dot(y.astype(BF16), plg_ref[...]))
        o_ref[...] = y + ple * gate


def _peer_ffn(x, u, vt_tiles, scores, aux, pe, plp, plg, ln_g, ln_b, tb=512):
    t = x.shape[0]
    row = lambda i, e: (i, 0)
    const = lambda i, e: (0, 0)
    return pl.pallas_call(
        _peer_ffn_body,
        grid=(t // tb, PEER_TILES),
        in_specs=[pl.BlockSpec((tb, D_MODEL), row),
                  pl.BlockSpec((PEER_TILE, D_MODEL), lambda i, e: (e, 0)),
                  pl.BlockSpec((1, D_MODEL, PEER_TILE), lambda i, e: (e, 0, 0)),
                  pl.BlockSpec((2 * R_HEADS, N_KEYS, tb), lambda i, e: (0, 0, i), pipeline_mode=pl.Buffered(1)),
                  pl.BlockSpec((R_HEADS, tb), lambda i, e: (0, i)),
                  pl.BlockSpec((tb, PLE_DIM), row),
                  pl.BlockSpec((PLE_DIM, D_MODEL), const, pipeline_mode=pl.Buffered(1)),
                  pl.BlockSpec((D_MODEL, D_MODEL), const, pipeline_mode=pl.Buffered(1)),
                  pl.BlockSpec((1, D_MODEL), const),
                  pl.BlockSpec((1, D_MODEL), const)],
        out_specs=pl.BlockSpec((tb, D_MODEL), row),
        out_shape=jax.ShapeDtypeStruct((t, D_MODEL), F32),
        scratch_shapes=[pltpu.VMEM((D_MODEL, tb), BF16),
                        pltpu.VMEM((D_MODEL, tb), F32),
                        pltpu.VMEM((PEER_TILE, tb), BF16)],
        compiler_params=_cparams(("parallel", "arbitrary")),
    )(x, u, vt_tiles, scores, aux, pe, plp, plg, ln_g, ln_b)


def _prep_layer(w_in, if_bias, norm_w, rel_bias, w_bm, w_ba, w_o, ln1g, ln1b, ln2g, ln2b,
                pq, psk, pu, pv, plp, plg):
    c = 0
    cols = {}
    for name, width in (("mq", M_WIDTH), ("mk", M_WIDTH), ("mv", M_WIDTH), ("mi", M_HEADS), ("mf", M_HEADS),
                        ("mo", M_WIDTH), ("aq", A_WIDTH), ("ak", A_WIDTH), ("av", A_WIDTH),
                        ("gm", D_MODEL), ("ga", D_MODEL)):
        cols[name] = w_in[:, c:c + width]
        c += width
    cat = lambda names: jnp.concatenate([cols[n] for n in names], axis=1)
    ones = lambda n: jnp.ones((1, n), F32)
    w_if = jnp.pad(cat(("mi", "mf")), ((0, 0), (0, LANES - 2 * M_HEADS)))
    b_if = jnp.pad(if_bias.reshape(1, 2 * M_HEADS).astype(F32), ((0, 0), (0, LANES - 2 * M_HEADS)))
    return dict(
        w_mqkv=cat(("mq", "mk", "mv")).astype(BF16),
        s_mqkv=jnp.concatenate([ones(M_WIDTH), ones(M_WIDTH) * (M_HDIM ** -0.5), ones(M_WIDTH)], axis=1),
        w_sig=cat(("mo", "gm", "ga")).astype(BF16), s_sig=ones(3 * D_MODEL),
        w_aqkv=cat(("aq", "ak", "av")).astype(BF16), s_aqkv=ones(3 * A_WIDTH),
        w_if=w_if, b_if=b_if,
        norm_w=norm_w.reshape(1, M_WIDTH).astype(F32),
        bias=_attn_bias(rel_bias),
        w_bm=w_bm.astype(BF16), w_ba=w_ba.astype(BF16), w_o=w_o.astype(BF16),
        ln1g=ln1g.reshape(1, D_MODEL), ln1b=ln1b.reshape(1, D_MODEL),
        ln2g=ln2g.reshape(1, D_MODEL), ln2b=ln2b.reshape(1, D_MODEL),
        wq_t=pq.T.astype(BF16),
        subkeys=psk.reshape(2 * R_HEADS, N_KEYS, HALF_KEY).astype(BF16),
        u=pu.astype(BF16),
        vt=pv.reshape(PEER_TILES, PEER_TILE, D_MODEL).transpose(0, 2, 1).astype(BF16),
        plp=plp.astype(BF16), plg=plg.astype(BF16),
    )


def _layer(x3, pe3, lw, state, cache):
    b, s, _ = x3.shape
    t = b * s
    x = x3.reshape(t, D_MODEL)
    mqkv, = _proj(x, lw["w_mqkv"], lw["s_mqkv"], None, (BF16,), tb=1024, nb=1024)
    sig3, = _proj(x, lw["w_sig"], lw["s_sig"], "sigmoid", (BF16,), tb=1024, nb=1024)
    aqkv, akv32 = _proj(x, lw["w_aqkv"], lw["s_aqkv"], None, (BF16, F32), tb=1024, nb=512)
    gates = _gates(x, lw["w_if"], lw["b_if"])

    if state is None:
        c0 = jnp.zeros((b, M_HEADS, M_HDIM, M_HDIM), F32)
        n0 = jnp.zeros((b, 1, M_WIDTH), F32)
        m0 = jnp.zeros((b, 1, LANES), F32)
    else:
        c0 = state[0].astype(F32)
        n0 = state[1].astype(F32).reshape(b, 1, M_WIDTH)
        m0 = jnp.pad(state[2].astype(F32), ((0, 0), (0, LANES - M_HEADS))).reshape(b, 1, LANES)
    hg, c_new, n_new, m_new = _mlstm(mqkv.reshape(b, s, 3 * M_WIDTH), gates.reshape(b, s, LANES),
                                     sig3.reshape(b, s, 3 * D_MODEL), lw["norm_w"], c0, n0, m0)
    n_new = n_new.reshape(b, M_HEADS, M_HDIM)
    m_new = m_new.reshape(b, LANES)[:, :M_HEADS]

    a3 = aqkv.reshape(b, s, 3 * A_WIDTH)
    q, k, v = a3[..., :A_WIDTH], a3[..., A_WIDTH:2 * A_WIDTH], a3[..., 2 * A_WIDTH:]
    kv32 = akv32.reshape(b, s, 3 * A_WIDTH)
    if cache is None:
        hist_k = jnp.zeros((b, ATT_REACH, A_WIDTH), BF16)
        hist_v = hist_k
        first_valid_block = ATT_REACH // ATT_Q
        keep = min(ATT_REACH, s)
        k_out = kv32[:, s - keep:, A_WIDTH:2 * A_WIDTH].reshape(b, keep, A_HEADS, A_HDIM)
        v_out = kv32[:, s - keep:, 2 * A_WIDTH:].reshape(b, keep, A_HEADS, A_HDIM)
    else:
        hist_k = cache[0].reshape(b, ATT_REACH, A_WIDTH).astype(BF16)
        hist_v = cache[1].reshape(b, ATT_REACH, A_WIDTH).astype(BF16)
        first_valid_block = 0
        k_out = kv32[..., A_WIDTH:2 * A_WIDTH].reshape(b, s, A_HEADS, A_HDIM)
        v_out = kv32[..., 2 * A_WIDTH:].reshape(b, s, A_HEADS, A_HDIM)
    sq = -(-s // ATT_Q) * ATT_Q
    tail = ((0, 0), (0, sq - s), (0, 0))
    oa = _attn(jnp.pad(q, tail), jnp.pad(jnp.concatenate([hist_k, k], axis=1), tail),
               jnp.pad(jnp.concatenate([hist_v, v], axis=1), tail), lw["bias"], first_valid_block)
    oa = oa[:, :s].reshape(t, A_WIDTH)

    x1 = _mix(hg.reshape(t, M_WIDTH), oa, sig3, x, lw["w_bm"], lw["w_ba"], lw["w_o"], lw["ln1g"], lw["ln1b"])
    scores, aux = _peer_sel(x1, lw["wq_t"], lw["subkeys"])
    x3_out = _peer_ffn(x1, lw["u"], lw["vt"], scores, aux, pe3.reshape(t, PLE_DIM), lw["plp"], lw["plg"],
                       lw["ln2g"], lw["ln2b"])
    return x3_out.reshape(b, s, D_MODEL), (c_new, n_new, m_new), (k_out, v_out)


def kernel(x_prompt, x_sample, cache_attn_k, cache_attn_v, state_mlstm_C, state_mlstm_n, state_mlstm_m,
           p_prompt, p_sample, w_in, mlstm_if_bias, mlstm_norm_w, attn_rel_bias, w_branch_m, w_branch_a,
           w_out, ln1_g, ln1_b, ln2_g, ln2_b, peer_wq, peer_subkeys, peer_u, peer_v, ple_proj, ple_gate):
    layer_w = (w_in, mlstm_if_bias, mlstm_norm_w, attn_rel_bias, w_branch_m, w_branch_a, w_out,
               ln1_g, ln1_b, ln2_g, ln2_b, peer_wq, peer_subkeys, peer_u, peer_v, ple_proj, ple_gate)
    yp, ys = x_prompt, x_sample
    outs_p = [[] for _ in range(5)]
    outs_s = [[] for _ in range(5)]
    for i in range(DEPTH):
        lw = _prep_layer(*(w[i] for w in layer_w))
        yp, (cp, np_, mp), (kp, vp) = _layer(yp, p_prompt[i], lw, None, None)
        ys, (cs, ns, ms), (ks, vs) = _layer(
            ys, p_sample[i], lw, (state_mlstm_C[i], state_mlstm_n[i], state_mlstm_m[i]),
            (cache_attn_k[i], cache_attn_v[i]))
        for lst, val in zip(outs_p, (kp, vp, cp, np_, mp)):
            lst.append(val)
        for lst, val in zip(outs_s, (ks, vs, cs, ns, ms)):
            lst.append(val)
    st = lambda l: jnp.stack(l, axis=0)
    return (yp, ys) + tuple(st(l) for l in outs_p) + tuple(st(l) for l in outs_s)
```

```python
import functools
import math

import jax
import jax.numpy as jnp
from jax import lax
from jax.experimental import pallas as pl
from jax.experimental.pallas import tpu as pltpu

F32 = jnp.float32
BF16 = jnp.bfloat16

D_MODEL = 1024
CHUNK = 64
M_HEADS = 4
M_HDIM = 256
M_WIDTH = 1024
A_HEADS = 8
A_HDIM = 64
A_WIDTH = 512
PAST_CHUNKS = 8
ATT_REACH = 512
REL_CLIP = 128
R_HEADS = 8
N_KEYS = 128
N_EXPERTS = N_KEYS * N_KEYS
HALF_KEY = 128
TOPK = 16
PLE_DIM = 256
DEPTH = 2
ALPHA = (2 * DEPTH) ** 0.25
LN_EPS = 1e-5

LANES = 128
MLSTM_ROWS = 2
ATT_Q = 2 * CHUNK
ATT_WIN = ATT_REACH + ATT_Q
VMEM_LIMIT = 48 * 1024 * 1024
NEG_INF = float("-inf")


def _cparams(sem):
    return pltpu.CompilerParams(dimension_semantics=sem, vmem_limit_bytes=VMEM_LIMIT)


def _sigmoid(x):
    return 1.0 / (1.0 + jnp.exp(-x))


def _dot(a, b):
    return jnp.dot(a, b, preferred_element_type=F32)


def _dot_nt(a, b):
    return lax.dot_general(a, b, (((1,), (1,)), ((), ())), preferred_element_type=F32)


def _dot_tn(a, b):
    return lax.dot_general(a, b, (((0,), (0,)), ((), ())), preferred_element_type=F32)


def _layer_norm(y, g, b):
    mu = jnp.mean(y, axis=-1, keepdims=True)
    d = y - mu
    var = jnp.mean(d * d, axis=-1, keepdims=True)
    return d * lax.rsqrt(var + LN_EPS) * g + b


def _proj_body(x_ref, w_ref, s_ref, *o_refs, act):
    acc = _dot(x_ref[...].astype(BF16), w_ref[...]) * s_ref[...]
    if act == "sigmoid":
        acc = _sigmoid(acc)
    for o in o_refs:
        o[...] = acc.astype(o.dtype)


def _proj(x, w, scale, act, out_dtypes, tb=512, nb=512):
    t, k = x.shape
    n = w.shape[1]
    return pl.pallas_call(
        functools.partial(_proj_body, act=act),
        grid=(t // tb, n // nb),
        in_specs=[pl.BlockSpec((tb, k), lambda i, j: (i, 0)),
                  pl.BlockSpec((k, nb), lambda i, j: (0, j)),
                  pl.BlockSpec((1, nb), lambda i, j: (0, j))],
        out_specs=[pl.BlockSpec((tb, nb), lambda i, j: (i, j)) for _ in out_dtypes],
        out_shape=[jax.ShapeDtypeStruct((t, n), d) for d in out_dtypes],
        compiler_params=_cparams(("parallel", "arbitrary")),
    )(x, w, scale)


def _split3(v):
    hi = v.astype(BF16)
    r = v - hi.astype(F32)
    mid = r.astype(BF16)
    lo = (r - mid.astype(F32)).astype(BF16)
    return hi, mid, lo


def _gate_body(x_ref, w_ref, b_ref, o_ref):
    xh, xm, xl = _split3(x_ref[...])
    wh, wm, wl = _split3(w_ref[...])
    z = (_dot(xh, wh) + (_dot(xh, wm) + _dot(xm, wh))
         + (_dot(xh, wl) + _dot(xm, wm) + _dot(xl, wh))) + b_ref[...]
    col = lax.broadcasted_iota(jnp.int32, z.shape, 1)
    log_sig = jnp.minimum(z, 0.0) - jnp.log1p(jnp.exp(-jnp.abs(z)))
    o_ref[...] = jnp.where(col < M_HEADS, z, log_sig)


def _gates(x, w_if, b_if, tb=512):
    t, k = x.shape
    return pl.pallas_call(
        _gate_body,
        grid=(t // tb,),
        in_specs=[pl.BlockSpec((tb, k), lambda i: (i, 0)),
                  pl.BlockSpec((k, LANES), lambda i: (0, 0)),
                  pl.BlockSpec((1, LANES), lambda i: (0, 0))],
        out_specs=pl.BlockSpec((tb, LANES), lambda i: (i, 0)),
        out_shape=jax.ShapeDtypeStruct((t, LANES), F32),
        compiler_params=_cparams(("parallel",)),
    )(x, w_if, b_if)


def _mlstm_body(qkv_ref, g_ref, og_ref, nw_ref, c0_ref, n0_ref, m0_ref,
                h_ref, c_ref, n_ref, m_ref):
    @pl.when(pl.program_id(1) == 0)
    def _():
        c_ref[...] = c0_ref[...]
        n_ref[...] = n0_ref[...]
        m_ref[...] = m0_ref[...]

    for bb in range(qkv_ref.shape[0]):
        _mlstm_chunk_update(bb, qkv_ref, g_ref, og_ref, nw_ref, h_ref, c_ref, n_ref, m_ref)


def _mlstm_chunk_update(bb, qkv_ref, g_ref, og_ref, nw_ref, h_ref, c_ref, n_ref, m_ref):
    L = CHUNK
    g = g_ref[bb]
    gt = jnp.concatenate([g, jnp.zeros((LANES - L, LANES), F32)], axis=0).T
    t_i = lax.broadcasted_iota(jnp.int32, (L, L), 0)
    s_i = lax.broadcasted_iota(jnp.int32, (L, L), 1)
    causal = s_i <= t_i
    lane = lax.broadcasted_iota(jnp.int32, (1, LANES), 1)
    m_all = m_ref[bb]
    m_next = m_all
    for h in range(M_HEADS):
        sl = slice(h * M_HDIM, (h + 1) * M_HDIM)
        q = qkv_ref[bb, :, h * M_HDIM:(h + 1) * M_HDIM]
        k = qkv_ref[bb, :, M_WIDTH + h * M_HDIM:M_WIDTH + (h + 1) * M_HDIM]
        v = qkv_ref[bb, :, 2 * M_WIDTH + h * M_HDIM:2 * M_WIDTH + (h + 1) * M_HDIM]
        ig_col = g[:, h:h + 1]
        lf_col = g[:, M_HEADS + h:M_HEADS + h + 1]
        ig_row = gt[h:h + 1, :L]
        lf_row = gt[M_HEADS + h:M_HEADS + h + 1, :L]
        b_col = jnp.sum(jnp.where(causal, lf_row, 0.0), axis=1, keepdims=True)
        b_row = jnp.sum(jnp.where(t_i <= s_i, lf_col, 0.0), axis=0, keepdims=True)
        m_prev = jnp.sum(jnp.where(lane == h, m_all, 0.0), axis=1, keepdims=True)
        inter = b_col + m_prev
        dmat = jnp.where(causal, b_col - b_row + ig_row, NEG_INF)
        m_t = jnp.maximum(inter, jnp.max(dmat, axis=1, keepdims=True))
        w_inter = jnp.exp(inter - m_t)
        w_intra = jnp.exp(dmat - m_t)
        a = w_intra * _dot_nt(q, k)
        c_h = c_ref[bb, h]
        n_h = n_ref[bb, :, sl]
        qf = q.astype(F32)
        num = w_inter * _dot_nt(q, c_h.astype(BF16)) + _dot(a.astype(BF16), v)
        den = w_inter * jnp.sum(qf * n_h, axis=1, keepdims=True) + jnp.sum(a, axis=1, keepdims=True)
        hh = num * (1.0 / jnp.maximum(jnp.abs(den), jnp.exp(-m_t)))
        mu = jnp.mean(hh, axis=1, keepdims=True)
        dv = hh - mu
        var = jnp.mean(dv * dv, axis=1, keepdims=True)
        hn = dv * lax.rsqrt(var + LN_EPS) * nw_ref[:, sl]
        h_ref[bb, :, sl] = (og_ref[bb, :, sl].astype(F32) * hn).astype(h_ref.dtype)
        m_new = m_t[L - 1:L, :]
        b_last = b_col[L - 1:L, :]
        g_state = jnp.exp(b_last + m_prev - m_new)
        g_s = jnp.exp(b_last - b_col + ig_col - m_new)
        vg = (g_s * v.astype(F32)).astype(BF16)
        c_ref[bb, h] = g_state * c_h + _dot_tn(vg, k)
        n_ref[bb, :, sl] = g_state * n_h + jnp.sum(g_s * k.astype(F32), axis=0, keepdims=True)
        m_next = jnp.where(lane == h, m_new, m_next)
    m_ref[bb] = m_next


def _mlstm(qkv, gates, og3, norm_w, c0, n0, m0):
    b, s, _ = qkv.shape
    nc = s // CHUNK
    nb = MLSTM_ROWS
    return pl.pallas_call(
        _mlstm_body,
        grid=(b // nb, nc),
        in_specs=[pl.BlockSpec((nb, CHUNK, 3 * M_WIDTH), lambda i, c: (i, c, 0)),
                  pl.BlockSpec((nb, CHUNK, LANES), lambda i, c: (i, c, 0)),
                  pl.BlockSpec((nb, CHUNK, M_WIDTH), lambda i, c: (i, c, 0)),
                  pl.BlockSpec((1, M_WIDTH), lambda i, c: (0, 0)),
                  pl.BlockSpec((nb, M_HEADS, M_HDIM, M_HDIM), lambda i, c: (i, 0, 0, 0)),
                  pl.BlockSpec((nb, 1, M_WIDTH), lambda i, c: (i, 0, 0)),
                  pl.BlockSpec((nb, 1, LANES), lambda i, c: (i, 0, 0))],
        out_specs=[pl.BlockSpec((nb, CHUNK, M_WIDTH), lambda i, c: (i, c, 0)),
                   pl.BlockSpec((nb, M_HEADS, M_HDIM, M_HDIM), lambda i, c: (i, 0, 0, 0)),
                   pl.BlockSpec((nb, 1, M_WIDTH), lambda i, c: (i, 0, 0)),
                   pl.BlockSpec((nb, 1, LANES), lambda i, c: (i, 0, 0))],
        out_shape=[jax.ShapeDtypeStruct((b, s, M_WIDTH), BF16),
                   jax.ShapeDtypeStruct((b, M_HEADS, M_HDIM, M_HDIM), F32),
                   jax.ShapeDtypeStruct((b, 1, M_WIDTH), F32),
                   jax.ShapeDtypeStruct((b, 1, LANES), F32)],
        compiler_params=_cparams(("parallel", "arbitrary")),
    )(qkv, gates, og3, norm_w, c0, n0, m0)


def _attn_body(q_ref, k_ref, v_ref, bias_ref, o_ref, *, first_valid_block):
    p = pl.program_id(1)
    row0 = pl.multiple_of(p * ATT_Q, ATT_Q)
    kw = k_ref[0, pl.ds(row0, ATT_WIN), :]
    vw = v_ref[0, pl.ds(row0, ATT_WIN), :]
    key_j = lax.broadcasted_iota(jnp.int32, (ATT_Q, ATT_WIN), 1)
    key_ok = key_j >= (first_valid_block - p) * ATT_Q
    lane = lax.broadcasted_iota(jnp.int32, (1, LANES), 1)
    for pair in range(A_HEADS // 2):
        cs = slice(pair * LANES, (pair + 1) * LANES)
        qp = q_ref[0, :, cs]
        kp = kw[:, cs]
        vp = vw[:, cs]
        o_pair = jnp.zeros((ATT_Q, LANES), F32)
        for e in range(2):
            head = 2 * pair + e
            in_head = (lane >= e * A_HDIM) & (lane < (e + 1) * A_HDIM)
            qm = jnp.where(in_head, qp, jnp.zeros_like(qp))
            s = _dot_nt(qm, kp) * (A_HDIM ** -0.5) + bias_ref[head]
            s = jnp.where(key_ok, s, NEG_INF)
            m = jnp.max(s, axis=1, keepdims=True)
            pe = jnp.exp(s - m)
            l = jnp.sum(pe, axis=1, keepdims=True)
            vm = jnp.where(in_head, vp, jnp.zeros_like(vp))
            o_pair = o_pair + _dot(pe.astype(BF16), vm) * (1.0 / l)
        o_ref[0, :, cs] = o_pair.astype(o_ref.dtype)


def _attn(q, k_pad, v_pad, bias, first_valid_block):
    b, sq, _ = q.shape
    sk = k_pad.shape[1]
    return pl.pallas_call(
        functools.partial(_attn_body, first_valid_block=first_valid_block),
        grid=(b, sq // ATT_Q),
        in_specs=[pl.BlockSpec((1, ATT_Q, A_WIDTH), lambda i, p: (i, p, 0)),
                  pl.BlockSpec((1, sk, A_WIDTH), lambda i, p: (i, 0, 0)),
                  pl.BlockSpec((1, sk, A_WIDTH), lambda i, p: (i, 0, 0)),
                  pl.BlockSpec((A_HEADS, ATT_Q, ATT_WIN), lambda i, p: (0, 0, 0))],
        out_specs=pl.BlockSpec((1, ATT_Q, A_WIDTH), lambda i, p: (i, p, 0)),
        out_shape=jax.ShapeDtypeStruct((b, sq, A_WIDTH), BF16),
        compiler_params=_cparams(("parallel", "arbitrary")),
    )(q, k_pad, v_pad, bias)


def _attn_bias(rel_bias):
    i = jnp.arange(ATT_Q)[:, None]
    j = jnp.arange(ATT_WIN)[None, :]
    n_diag = ATT_Q + ATT_WIN
    k = jnp.arange(n_diag)
    e = rel_bias.astype(F32)[:, jnp.clip(ATT_REACH + ATT_Q - 1 - k, -REL_CLIP, REL_CLIP) + REL_CLIP]
    skew = jnp.tile(e, (1, ATT_Q))[:, :ATT_Q * (n_diag - 1)].reshape(A_HEADS, ATT_Q, n_diag - 1)
    bias = skew[:, :, ATT_Q - 1:ATT_Q - 1 + ATT_WIN]
    qc = i // CHUNK
    kc = j // CHUNK - PAST_CHUNKS
    visible = (kc <= qc) & (kc >= qc - PAST_CHUNKS)
    return jnp.where(visible[None], bias, NEG_INF)


def _mix_body(hg_ref, oa_ref, gm_ref, ga_ref, x_ref, wbm_ref, wba_ref, wo_ref, g_ref, b_ref, o_ref):
    ym = _dot(hg_ref[...], wbm_ref[...])
    ya = _dot(oa_ref[...], wba_ref[...])
    gated = gm_ref[...].astype(F32) * ym + ga_ref[...].astype(F32) * ya
    mix = _dot(gated.astype(BF16), wo_ref[...])
    o_ref[...] = _layer_norm(ALPHA * x_ref[...] + mix, g_ref[...], b_ref[...])


def _mix(hg, oa, sg3, x, wbm, wba, wo, ln_g, ln_b, tb=512):
    t = x.shape[0]
    row = lambda i: (i, 0)
    const = lambda i: (0, 0)
    return pl.pallas_call(
        _mix_body,
        grid=(t // tb,),
        in_specs=[pl.BlockSpec((tb, M_WIDTH), row),
                  pl.BlockSpec((tb, A_WIDTH), row),
                  pl.BlockSpec((tb, D_MODEL), lambda i: (i, 1)),
                  pl.BlockSpec((tb, D_MODEL), lambda i: (i, 2)),
                  pl.BlockSpec((tb, D_MODEL), row),
                  pl.BlockSpec((M_WIDTH, D_MODEL), const),
                  pl.BlockSpec((A_WIDTH, D_MODEL), const),
                  pl.BlockSpec((D_MODEL, D_MODEL), const),
                  pl.BlockSpec((1, D_MODEL), const),
                  pl.BlockSpec((1, D_MODEL), const)],
        out_specs=pl.BlockSpec((tb, D_MODEL), row),
        out_shape=jax.ShapeDtypeStruct((t, D_MODEL), F32),
        compiler_params=_cparams(("parallel",)),
    )(hg, oa, sg3, sg3, x, wbm, wba, wo, ln_g, ln_b)


SUBLANES = 8
LOG2E = math.log2(math.e)


def _batcher_network(lo, hi):
    def merge(lo, hi, r):
        step = r * 2
        if step < hi - lo:
            yield from merge(lo, hi, step)
            yield from merge(lo + r, hi, step)
            yield from [(i, i + r) for i in range(lo + r, hi - r, step)]
        else:
            yield (lo, lo + r)
    if hi - lo >= 1:
        mid = lo + (hi - lo) // 2
        yield from _batcher_network(lo, mid)
        yield from _batcher_network(mid + 1, hi)
        yield from merge(lo, hi, 1)


_SORT16 = tuple(_batcher_network(0, TOPK - 1))
_BITONIC16 = tuple((i, i + s) for s in (8, 4, 2, 1) for i in range(TOPK) if (i // s) % 2 == 0)


def _compare_exchange(rows, net):
    rows = list(rows)
    for i, j in net:
        a, b = rows[i], rows[j]
        if b is None:
            continue
        if a is None:
            rows[i], rows[j] = b, None
        else:
            rows[i], rows[j] = jnp.maximum(a, b), jnp.minimum(a, b)
    return rows


def _top16_values(slabs):
    rows = list(slabs) + [None] * (TOPK - len(slabs))
    rows = _compare_exchange(rows, _SORT16)
    for shift in (4, 2, 1):
        merged = []
        for r in range(TOPK):
            a, b = rows[r], rows[TOPK - 1 - r]
            b = None if b is None else pltpu.roll(b, shift, axis=0)
            merged.append(b if a is None else a if b is None else jnp.maximum(a, b))
        rows = _compare_exchange(merged, _BITONIC16)
    return rows


def _by_sublane(rows, sub):
    out = rows[SUBLANES - 1]
    for g in range(SUBLANES - 2, -1, -1):
        out = jnp.where(sub == g, rows[g], out)
    return out


def _peer_sel_body(x_ref, wq_ref, sk_ref, s_ref, aux_ref):
    tb = x_ref.shape[0]
    qt = _dot_nt(wq_ref[...], x_ref[...].astype(BF16))
    sub = lax.broadcasted_iota(jnp.int32, (SUBLANES, tb), 0)
    for h in range(R_HEADS):
        top = []
        for p in range(2):
            r = h * 2 + p
            qhp = qt[r * HALF_KEY:(r + 1) * HALF_KEY, :].astype(BF16)
            st = _dot(sk_ref[r], qhp) * LOG2E
            s_ref[r] = st
            top.append(_top16_values([st[g * SUBLANES:(g + 1) * SUBLANES, :] for g in range(N_KEYS // SUBLANES)]))
        c1, c2 = top
        v1_lo, v1_hi = _by_sublane(c1[:SUBLANES], sub), _by_sublane(c1[SUBLANES:], sub)
        v2_hi = _by_sublane(c2[SUBLANES:], sub)
        firsts = [v1_lo] * SUBLANES + [c1[0], v1_hi]
        seconds = c2[:SUBLANES] + [v2_hi, c2[0]]
        cand = [f + s for f, s in zip(firsts, seconds)]
        cv = _top16_values(cand)
        tau, best = cv[TOPK - 1], cv[0]
        picked = [c >= tau for c in cand]
        z = sum(jnp.where(pk, jnp.exp2(c - best), 0.0) for pk, c in zip(picked, cand))
        shift = best[0:1, :] + jnp.log2(jnp.sum(z, axis=0, keepdims=True))
        shifted = [jnp.where(pk, (f - shift) + s, jnp.inf) for pk, f, s in zip(picked, firsts, seconds)]
        tau_shifted = functools.reduce(jnp.minimum, shifted)
        s_ref[2 * h] = s_ref[2 * h] - shift
        aux_ref[h:h + 1, :] = jnp.min(tau_shifted, axis=0, keepdims=True)


def _peer_sel(x, wq_t, subkeys, tb=256):
    t = x.shape[0]
    return pl.pallas_call(
        _peer_sel_body,
        grid=(t // tb,),
        in_specs=[pl.BlockSpec((tb, D_MODEL), lambda i: (i, 0)),
                  pl.BlockSpec((2 * R_HEADS * HALF_KEY, D_MODEL), lambda i: (0, 0)),
                  pl.BlockSpec((2 * R_HEADS, N_KEYS, HALF_KEY), lambda i: (0, 0, 0))],
        out_specs=[pl.BlockSpec((2 * R_HEADS, N_KEYS, tb), lambda i: (0, 0, i)),
                   pl.BlockSpec((R_HEADS, tb), lambda i: (0, i))],
        out_shape=[jax.ShapeDtypeStruct((2 * R_HEADS, N_KEYS, t), F32),
                   jax.ShapeDtypeStruct((R_HEADS, t), F32)],
        compiler_params=_cparams(("parallel",)),
    )(x, wq_t, subkeys)


def _gelu_tanh(x):
    return 0.5 * x * (1.0 + jnp.tanh(math.sqrt(2.0 / math.pi) * (x + 0.044715 * (x * x * x))))


PEER_TILE = SUBLANES * N_KEYS
PEER_TILES = N_EXPERTS // PEER_TILE
PEER_PARTS = 2


def _peer_ffn_body(x_ref, u_ref, vt_ref, s_ref, aux_ref, pe_ref, plp_ref, plg_ref, g_ref, b_ref,
                   o_ref, xb_scr, acc_scr, p_scr):
    e = pl.program_id(1)
    tb = x_ref.shape[0]

    @pl.when(e == 0)
    def _():
        xb_scr[...] = x_ref[...].astype(BF16)
        acc_scr[...] = jnp.zeros_like(acc_scr)

    i0 = pl.multiple_of(e * SUBLANES, SUBLANES)
    keys_per_part = SUBLANES // PEER_PARTS
    part = keys_per_part * N_KEYS
    scores = [_dot_nt(u_ref[pt * part:(pt + 1) * part, :], xb_scr[...]).astype(BF16) for pt in range(PEER_PARTS)]
    for pt in range(PEER_PARTS):
        es = slice(pt * part, (pt + 1) * part)
        act = _gelu_tanh(scores[pt])
        for ii in range(keys_per_part):
            rs = slice(ii * N_KEYS, (ii + 1) * N_KEYS)
            for lg in range(tb // LANES):
                ls = slice(lg * LANES, (lg + 1) * LANES)
                w = jnp.zeros((N_KEYS, LANES), F32)
                for h in range(R_HEADS):
                    s1 = s_ref[2 * h, pl.ds(i0, SUBLANES), ls][pt * keys_per_part + ii:pt * keys_per_part + ii + 1, :]
                    s = s1 + s_ref[2 * h + 1, :, ls]
                    w = w + jnp.where(s >= aux_ref[h:h + 1, ls], jnp.exp2(s), 0.0)
                p_scr[pt * part + ii * N_KEYS:pt * part + (ii + 1) * N_KEYS, ls] = act[rs, ls] * w.astype(BF16)
        acc_scr[...] += _dot(vt_ref[0, :, es], p_scr[es, :])

    @pl.when(e == pl.num_programs(1) - 1)
    def _():
        x = x_ref[...]
        y = _layer_norm(ALPHA * x + acc_scr[...].T, g_ref[...], b_ref[...])
        ple = _dot(pe_ref[...].astype(BF16), plp_ref[...])
        gate = _sigmoid(_dot(y.astype(BF16), plg_ref[...]))
        o_ref[...] = y + ple * gate


def _peer_ffn(x, u, vt_tiles, scores, aux, pe, plp, plg, ln_g, ln_b, tb=512):
    t = x.shape[0]
    row = lambda i, e: (i, 0)
    const = lambda i, e: (0, 0)
    return pl.pallas_call(
        _peer_ffn_body,
        grid=(t // tb, PEER_TILES),
        in_specs=[pl.BlockSpec((tb, D_MODEL), row),
                  pl.BlockSpec((PEER_TILE, D_MODEL), lambda i, e: (e, 0)),
                  pl.BlockSpec((1, D_MODEL, PEER_TILE), lambda i, e: (e, 0, 0)),
                  pl.BlockSpec((2 * R_HEADS, N_KEYS, tb), lambda i, e: (0, 0, i)),
                  pl.BlockSpec((R_HEADS, tb), lambda i, e: (0, i)),
                  pl.BlockSpec((tb, PLE_DIM), row),
                  pl.BlockSpec((PLE_DIM, D_MODEL), const),
                  pl.BlockSpec((D_MODEL, D_MODEL), const),
                  pl.BlockSpec((1, D_MODEL), const),
                  pl.BlockSpec((1, D_MODEL), const)],
        out_specs=pl.BlockSpec((tb, D_MODEL), row),
        out_shape=jax.ShapeDtypeStruct((t, D_MODEL), F32),
        scratch_shapes=[pltpu.VMEM((tb, D_MODEL), BF16),
                        pltpu.VMEM((D_MODEL, tb), F32),
                        pltpu.VMEM((PEER_TILE, tb), BF16)],
        compiler_params=_cparams(("parallel", "arbitrary")),
    )(x, u, vt_tiles, scores, aux, pe, plp, plg, ln_g, ln_b)


def _prep_layer(w_in, if_bias, norm_w, rel_bias, w_bm, w_ba, w_o, ln1g, ln1b, ln2g, ln2b,
                pq, psk, pu, pv, plp, plg):
    c = 0
    cols = {}
    for name, width in (("mq", M_WIDTH), ("mk", M_WIDTH), ("mv", M_WIDTH), ("mi", M_HEADS), ("mf", M_HEADS),
                        ("mo", M_WIDTH), ("aq", A_WIDTH), ("ak", A_WIDTH), ("av", A_WIDTH),
                        ("gm", D_MODEL), ("ga", D_MODEL)):
        cols[name] = w_in[:, c:c + width]
        c += width
    cat = lambda names: jnp.concatenate([cols[n] for n in names], axis=1)
    ones = lambda n: jnp.ones((1, n), F32)
    w_if = jnp.pad(cat(("mi", "mf")), ((0, 0), (0, LANES - 2 * M_HEADS)))
    b_if = jnp.pad(if_bias.reshape(1, 2 * M_HEADS).astype(F32), ((0, 0), (0, LANES - 2 * M_HEADS)))
    return dict(
        w_mqkv=cat(("mq", "mk", "mv")).astype(BF16),
        s_mqkv=jnp.concatenate([ones(M_WIDTH), ones(M_WIDTH) * (M_HDIM ** -0.5), ones(M_WIDTH)], axis=1),
        w_sig=cat(("mo", "gm", "ga")).astype(BF16), s_sig=ones(3 * D_MODEL),
        w_aqkv=cat(("aq", "ak", "av")).astype(BF16), s_aqkv=ones(3 * A_WIDTH),
        w_if=w_if, b_if=b_if,
        norm_w=norm_w.reshape(1, M_WIDTH).astype(F32),
        bias=_attn_bias(rel_bias),
        w_bm=w_bm.astype(BF16), w_ba=w_ba.astype(BF16), w_o=w_o.astype(BF16),
        ln1g=ln1g.reshape(1, D_MODEL), ln1b=ln1b.reshape(1, D_MODEL),
        ln2g=ln2g.reshape(1, D_MODEL), ln2b=ln2b.reshape(1, D_MODEL),
        wq_t=pq.T.astype(BF16),
        subkeys=psk.reshape(2 * R_HEADS, N_KEYS, HALF_KEY).astype(BF16),
        u=pu.astype(BF16),
        vt=pv.reshape(PEER_TILES, PEER_TILE, D_MODEL).transpose(0, 2, 1).astype(BF16),
        plp=plp.astype(BF16), plg=plg.astype(BF16),
    )


def _layer(x3, pe3, lw, state, cache):
    b, s, _ = x3.shape
    t = b * s
    x = x3.reshape(t, D_MODEL)
    mqkv, = _proj(x, lw["w_mqkv"], lw["s_mqkv"], None, (BF16,), tb=1024, nb=1024)
    sig3, = _proj(x, lw["w_sig"], lw["s_sig"], "sigmoid", (BF16,), tb=1024, nb=1024)
    aqkv, akv32 = _proj(x, lw["w_aqkv"], lw["s_aqkv"], None, (BF16, F32), tb=1024, nb=512)
    gates = _gates(x, lw["w_if"], lw["b_if"])

    if state is None:
        c0 = jnp.zeros((b, M_HEADS, M_HDIM, M_HDIM), F32)
        n0 = jnp.zeros((b, 1, M_WIDTH), F32)
        m0 = jnp.zeros((b, 1, LANES), F32)
    else:
        c0 = state[0].astype(F32)
        n0 = state[1].astype(F32).reshape(b, 1, M_WIDTH)
        m0 = jnp.pad(state[2].astype(F32), ((0, 0), (0, LANES - M_HEADS))).reshape(b, 1, LANES)
    hg, c_new, n_new, m_new = _mlstm(mqkv.reshape(b, s, 3 * M_WIDTH), gates.reshape(b, s, LANES),
                                     sig3.reshape(b, s, 3 * D_MODEL), lw["norm_w"], c0, n0, m0)
    n_new = n_new.reshape(b, M_HEADS, M_HDIM)
    m_new = m_new.reshape(b, LANES)[:, :M_HEADS]

    a3 = aqkv.reshape(b, s, 3 * A_WIDTH)
    q, k, v = a3[..., :A_WIDTH], a3[..., A_WIDTH:2 * A_WIDTH], a3[..., 2 * A_WIDTH:]
    kv32 = akv32.reshape(b, s, 3 * A_WIDTH)
    if cache is None:
        hist_k = jnp.zeros((b, ATT_REACH, A_WIDTH), BF16)
        hist_v = hist_k
        first_valid_block = ATT_REACH // ATT_Q
        keep = min(ATT_REACH, s)
        k_out = kv32[:, s - keep:, A_WIDTH:2 * A_WIDTH].reshape(b, keep, A_HEADS, A_HDIM)
        v_out = kv32[:, s - keep:, 2 * A_WIDTH:].reshape(b, keep, A_HEADS, A_HDIM)
    else:
        hist_k = cache[0].reshape(b, ATT_REACH, A_WIDTH).astype(BF16)
        hist_v = cache[1].reshape(b, ATT_REACH, A_WIDTH).astype(BF16)
        first_valid_block = 0
        k_out = kv32[..., A_WIDTH:2 * A_WIDTH].reshape(b, s, A_HEADS, A_HDIM)
        v_out = kv32[..., 2 * A_WIDTH:].reshape(b, s, A_HEADS, A_HDIM)
    sq = -(-s // ATT_Q) * ATT_Q
    tail = ((0, 0), (0, sq - s), (0, 0))
    oa = _attn(jnp.pad(q, tail), jnp.pad(jnp.concatenate([hist_k, k], axis=1), tail),
               jnp.pad(jnp.concatenate([hist_v, v], axis=1), tail), lw["bias"], first_valid_block)
    oa = oa[:, :s].reshape(t, A_WIDTH)

    x1 = _mix(hg.reshape(t, M_WIDTH), oa, sig3, x, lw["w_bm"], lw["w_ba"], lw["w_o"], lw["ln1g"], lw["ln1b"])
    scores, aux = _peer_sel(x1, lw["wq_t"], lw["subkeys"])
    x3_out = _peer_ffn(x1, lw["u"], lw["vt"], scores, aux, pe3.reshape(t, PLE_DIM), lw["plp"], lw["plg"],
                       lw["ln2g"], lw["ln2b"])
    return x3_out.reshape(b, s, D_MODEL), (c_new, n_new, m_new), (k_out, v_out)


def kernel(x_prompt, x_sample, cache_attn_k, cache_attn_v, state_mlstm_C, state_mlstm_n, state_mlstm_m,
           p_prompt, p_sample, w_in, mlstm_if_bias, mlstm_norm_w, attn_rel_bias, w_branch_m, w_branch_a,
           w_out, ln1_g, ln1_b, ln2_g, ln2_b, peer_wq, peer_subkeys, peer_u, peer_v, ple_proj, ple_gate):
    layer_w = (w_in, mlstm_if_bias, mlstm_norm_w, attn_rel_bias, w_branch_m, w_branch_a, w_out,
               ln1_g, ln1_b, ln2_g, ln2_b, peer_wq, peer_subkeys, peer_u, peer_v, ple_proj, ple_gate)
    yp, ys = x_prompt, x_sample
    outs_p = [[] for _ in range(5)]
    outs_s = [[] for _ in range(5)]
    for i in range(DEPTH):
        lw = _prep_layer(*(w[i] for w in layer_w))
        yp, (cp, np_, mp), (kp, vp) = _layer(yp, p_prompt[i], lw, None, None)
        ys, (cs, ns, ms), (ks, vs) = _layer(
            ys, p_sample[i], lw, (state_mlstm_C[i], state_mlstm_n[i], state_mlstm_m[i]),
            (cache_attn_k[i], cache_attn_v[i]))
        for lst, val in zip(outs_p, (kp, vp, cp, np_, mp)):
            lst.append(val)
        for lst, val in zip(outs_s, (ks, vs, cs, ns, ms)):
            lst.append(val)
    st = lambda l: jnp.stack(l, axis=0)
    return (yp, ys) + tuple(st(l) for l in outs_p) + tuple(st(l) for l in outs_s)
```

```python
import functools
import math

import jax
import jax.numpy as jnp
from jax import lax
from jax.experimental import pallas as pl
from jax.experimental.pallas import tpu as pltpu

F32 = jnp.float32
BF16 = jnp.bfloat16

D_MODEL = 1024
CHUNK = 64
M_HEADS = 4
M_HDIM = 256
M_WIDTH = 1024
A_HEADS = 8
A_HDIM = 64
A_WIDTH = 512
PAST_CHUNKS = 8
ATT_REACH = 512
REL_CLIP = 128
R_HEADS = 8
N_KEYS = 128
N_EXPERTS = N_KEYS * N_KEYS
HALF_KEY = 128
TOPK = 16
PLE_DIM = 256
DEPTH = 2
ALPHA = (2 * DEPTH) ** 0.25
LN_EPS = 1e-5

LANES = 128
MLSTM_ROWS = 2
ATT_Q = 2 * CHUNK
ATT_WIN = ATT_REACH + ATT_Q
VMEM_LIMIT = 48 * 1024 * 1024
NEG_INF = float("-inf")


def _cparams(sem):
    return pltpu.CompilerParams(dimension_semantics=sem, vmem_limit_bytes=VMEM_LIMIT)


def _sigmoid(x):
    return 1.0 / (1.0 + jnp.exp(-x))


def _dot(a, b):
    return jnp.dot(a, b, preferred_element_type=F32)


def _dot_nt(a, b):
    return lax.dot_general(a, b, (((1,), (1,)), ((), ())), preferred_element_type=F32)


def _dot_tn(a, b):
    return lax.dot_general(a, b, (((0,), (0,)), ((), ())), preferred_element_type=F32)


def _layer_norm(y, g, b):
    mu = jnp.mean(y, axis=-1, keepdims=True)
    d = y - mu
    var = jnp.mean(d * d, axis=-1, keepdims=True)
    return d * lax.rsqrt(var + LN_EPS) * g + b


def _proj_body(x_ref, w_ref, s_ref, *o_refs, act):
    acc = _dot(x_ref[...].astype(BF16), w_ref[...]) * s_ref[...]
    if act == "sigmoid":
        acc = _sigmoid(acc)
    for o in o_refs:
        o[...] = acc.astype(o.dtype)


def _proj(x, w, scale, act, out_dtypes, tb=512, nb=512):
    t, k = x.shape
    n = w.shape[1]
    return pl.pallas_call(
        functools.partial(_proj_body, act=act),
        grid=(t // tb, n // nb),
        in_specs=[pl.BlockSpec((tb, k), lambda i, j: (i, 0)),
                  pl.BlockSpec((k, nb), lambda i, j: (0, j)),
                  pl.BlockSpec((1, nb), lambda i, j: (0, j))],
        out_specs=[pl.BlockSpec((tb, nb), lambda i, j: (i, j)) for _ in out_dtypes],
        out_shape=[jax.ShapeDtypeStruct((t, n), d) for d in out_dtypes],
        compiler_params=_cparams(("parallel", "arbitrary")),
    )(x, w, scale)


def _split3(v):
    hi = v.astype(BF16)
    r = v - hi.astype(F32)
    mid = r.astype(BF16)
    lo = (r - mid.astype(F32)).astype(BF16)
    return hi, mid, lo


def _gate_body(x_ref, w_ref, b_ref, o_ref):
    xh, xm, xl = _split3(x_ref[...])
    wh, wm, wl = _split3(w_ref[...])
    z = (_dot(xh, wh) + (_dot(xh, wm) + _dot(xm, wh))
         + (_dot(xh, wl) + _dot(xm, wm) + _dot(xl, wh))) + b_ref[...]
    col = lax.broadcasted_iota(jnp.int32, z.shape, 1)
    log_sig = jnp.minimum(z, 0.0) - jnp.log1p(jnp.exp(-jnp.abs(z)))
    o_ref[...] = jnp.where(col < M_HEADS, z, log_sig)


def _gates(x, w_if, b_if, tb=512):
    t, k = x.shape
    return pl.pallas_call(
        _gate_body,
        grid=(t // tb,),
        in_specs=[pl.BlockSpec((tb, k), lambda i: (i, 0)),
                  pl.BlockSpec((k, LANES), lambda i: (0, 0)),
                  pl.BlockSpec((1, LANES), lambda i: (0, 0))],
        out_specs=pl.BlockSpec((tb, LANES), lambda i: (i, 0)),
        out_shape=jax.ShapeDtypeStruct((t, LANES), F32),
        compiler_params=_cparams(("parallel",)),
    )(x, w_if, b_if)


def _mlstm_body(qkv_ref, g_ref, og_ref, nw_ref, c0_ref, n0_ref, m0_ref,
                h_ref, c_ref, n_ref, m_ref):
    @pl.when(pl.program_id(1) == 0)
    def _():
        c_ref[...] = c0_ref[...]
        n_ref[...] = n0_ref[...]
        m_ref[...] = m0_ref[...]

    for bb in range(qkv_ref.shape[0]):
        _mlstm_chunk_update(bb, qkv_ref, g_ref, og_ref, nw_ref, h_ref, c_ref, n_ref, m_ref)


def _mlstm_chunk_update(bb, qkv_ref, g_ref, og_ref, nw_ref, h_ref, c_ref, n_ref, m_ref):
    L = CHUNK
    g = g_ref[bb]
    gt = jnp.concatenate([g, jnp.zeros((LANES - L, LANES), F32)], axis=0).T
    t_i = lax.broadcasted_iota(jnp.int32, (L, L), 0)
    s_i = lax.broadcasted_iota(jnp.int32, (L, L), 1)
    causal = s_i <= t_i
    lane = lax.broadcasted_iota(jnp.int32, (1, LANES), 1)
    m_all = m_ref[bb]
    m_next = m_all
    for h in range(M_HEADS):
        sl = slice(h * M_HDIM, (h + 1) * M_HDIM)
        q = qkv_ref[bb, :, h * M_HDIM:(h + 1) * M_HDIM]
        k = qkv_ref[bb, :, M_WIDTH + h * M_HDIM:M_WIDTH + (h + 1) * M_HDIM]
        v = qkv_ref[bb, :, 2 * M_WIDTH + h * M_HDIM:2 * M_WIDTH + (h + 1) * M_HDIM]
        ig_col = g[:, h:h + 1]
        lf_col = g[:, M_HEADS + h:M_HEADS + h + 1]
        ig_row = gt[h:h + 1, :L]
        lf_row = gt[M_HEADS + h:M_HEADS + h + 1, :L]
        b_col = jnp.sum(jnp.where(causal, lf_row, 0.0), axis=1, keepdims=True)
        b_row = jnp.sum(jnp.where(t_i <= s_i, lf_col, 0.0), axis=0, keepdims=True)
        m_prev = jnp.sum(jnp.where(lane == h, m_all, 0.0), axis=1, keepdims=True)
        inter = b_col + m_prev
        dmat = jnp.where(causal, b_col - b_row + ig_row, NEG_INF)
        m_t = jnp.maximum(inter, jnp.max(dmat, axis=1, keepdims=True))
        w_inter = jnp.exp(inter - m_t)
        w_intra = jnp.exp(dmat - m_t)
        a = w_intra * _dot_nt(q, k)
        c_h = c_ref[bb, h]
        n_h = n_ref[bb, :, sl]
        qf = q.astype(F32)
        num = w_inter * _dot_nt(q, c_h.astype(BF16)) + _dot(a.astype(BF16), v)
        den = w_inter * jnp.sum(qf * n_h, axis=1, keepdims=True) + jnp.sum(a, axis=1, keepdims=True)
        hh = num * (1.0 / jnp.maximum(jnp.abs(den), jnp.exp(-m_t)))
        mu = jnp.mean(hh, axis=1, keepdims=True)
        dv = hh - mu
        var = jnp.mean(dv * dv, axis=1, keepdims=True)
        hn = dv * lax.rsqrt(var + LN_EPS) * nw_ref[:, sl]
        h_ref[bb, :, sl] = (og_ref[bb, :, sl].astype(F32) * hn).astype(h_ref.dtype)
        m_new = m_t[L - 1:L, :]
        b_last = b_col[L - 1:L, :]
        g_state = jnp.exp(b_last + m_prev - m_new)
        g_s = jnp.exp(b_last - b_col + ig_col - m_new)
        vg = (g_s * v.astype(F32)).astype(BF16)
        c_ref[bb, h] = g_state * c_h + _dot_tn(vg, k)
        n_ref[bb, :, sl] = g_state * n_h + jnp.sum(g_s * k.astype(F32), axis=0, keepdims=True)
        m_next = jnp.where(lane == h, m_new, m_next)
    m_ref[bb] = m_next


def _mlstm(qkv, gates, og3, norm_w, c0, n0, m0):
    b, s, _ = qkv.shape
    nc = s // CHUNK
    nb = MLSTM_ROWS
    return pl.pallas_call(
        _mlstm_body,
        grid=(b // nb, nc),
        in_specs=[pl.BlockSpec((nb, CHUNK, 3 * M_WIDTH), lambda i, c: (i, c, 0)),
                  pl.BlockSpec((nb, CHUNK, LANES), lambda i, c: (i, c, 0)),
                  pl.BlockSpec((nb, CHUNK, M_WIDTH), lambda i, c: (i, c, 0)),
                  pl.BlockSpec((1, M_WIDTH), lambda i, c: (0, 0)),
                  pl.BlockSpec((nb, M_HEADS, M_HDIM, M_HDIM), lambda i, c: (i, 0, 0, 0)),
                  pl.BlockSpec((nb, 1, M_WIDTH), lambda i, c: (i, 0, 0)),
                  pl.BlockSpec((nb, 1, LANES), lambda i, c: (i, 0, 0))],
        out_specs=[pl.BlockSpec((nb, CHUNK, M_WIDTH), lambda i, c: (i, c, 0)),
                   pl.BlockSpec((nb, M_HEADS, M_HDIM, M_HDIM), lambda i, c: (i, 0, 0, 0)),
                   pl.BlockSpec((nb, 1, M_WIDTH), lambda i, c: (i, 0, 0)),
                   pl.BlockSpec((nb, 1, LANES), lambda i, c: (i, 0, 0))],
        out_shape=[jax.ShapeDtypeStruct((b, s, M_WIDTH), BF16),
                   jax.ShapeDtypeStruct((b, M_HEADS, M_HDIM, M_HDIM), F32),
                   jax.ShapeDtypeStruct((b, 1, M_WIDTH), F32),
                   jax.ShapeDtypeStruct((b, 1, LANES), F32)],
        compiler_params=_cparams(("parallel", "arbitrary")),
    )(qkv, gates, og3, norm_w, c0, n0, m0)


def _attn_body(q_ref, k_ref, v_ref, bias_ref, o_ref, *, first_valid_block):
    p = pl.program_id(1)
    row0 = pl.multiple_of(p * ATT_Q, ATT_Q)
    kw = k_ref[0, pl.ds(row0, ATT_WIN), :]
    vw = v_ref[0, pl.ds(row0, ATT_WIN), :]
    key_j = lax.broadcasted_iota(jnp.int32, (ATT_Q, ATT_WIN), 1)
    key_ok = key_j >= (first_valid_block - p) * ATT_Q
    lane = lax.broadcasted_iota(jnp.int32, (1, LANES), 1)
    for pair in range(A_HEADS // 2):
        cs = slice(pair * LANES, (pair + 1) * LANES)
        qp = q_ref[0, :, cs]
        kp = kw[:, cs]
        vp = vw[:, cs]
        o_pair = jnp.zeros((ATT_Q, LANES), F32)
        for e in range(2):
            head = 2 * pair + e
            in_head = (lane >= e * A_HDIM) & (lane < (e + 1) * A_HDIM)
            qm = jnp.where(in_head, qp, jnp.zeros_like(qp))
            s = _dot_nt(qm, kp) * (A_HDIM ** -0.5) + bias_ref[head]
            s = jnp.where(key_ok, s, NEG_INF)
            m = jnp.max(s, axis=1, keepdims=True)
            pe = jnp.exp(s - m)
            l = jnp.sum(pe, axis=1, keepdims=True)
            vm = jnp.where(in_head, vp, jnp.zeros_like(vp))
            o_pair = o_pair + _dot(pe.astype(BF16), vm) * (1.0 / l)
        o_ref[0, :, cs] = o_pair.astype(o_ref.dtype)


def _attn(q, k_pad, v_pad, bias, first_valid_block):
    b, sq, _ = q.shape
    sk = k_pad.shape[1]
    return pl.pallas_call(
        functools.partial(_attn_body, first_valid_block=first_valid_block),
        grid=(b, sq // ATT_Q),
        in_specs=[pl.BlockSpec((1, ATT_Q, A_WIDTH), lambda i, p: (i, p, 0)),
                  pl.BlockSpec((1, sk, A_WIDTH), lambda i, p: (i, 0, 0)),
                  pl.BlockSpec((1, sk, A_WIDTH), lambda i, p: (i, 0, 0)),
                  pl.BlockSpec((A_HEADS, ATT_Q, ATT_WIN), lambda i, p: (0, 0, 0))],
        out_specs=pl.BlockSpec((1, ATT_Q, A_WIDTH), lambda i, p: (i, p, 0)),
        out_shape=jax.ShapeDtypeStruct((b, sq, A_WIDTH), BF16),
        compiler_params=_cparams(("parallel", "arbitrary")),
    )(q, k_pad, v_pad, bias)


def _attn_bias(rel_bias):
    i = jnp.arange(ATT_Q)[:, None]
    j = jnp.arange(ATT_WIN)[None, :]
    n_diag = ATT_Q + ATT_WIN
    k = jnp.arange(n_diag)
    e = rel_bias.astype(F32)[:, jnp.clip(ATT_REACH + ATT_Q - 1 - k, -REL_CLIP, REL_CLIP) + REL_CLIP]
    skew = jnp.tile(e, (1, ATT_Q))[:, :ATT_Q * (n_diag - 1)].reshape(A_HEADS, ATT_Q, n_diag - 1)
    bias = skew[:, :, ATT_Q - 1:ATT_Q - 1 + ATT_WIN]
    qc = i // CHUNK
    kc = j // CHUNK - PAST_CHUNKS
    visible = (kc <= qc) & (kc >= qc - PAST_CHUNKS)
    return jnp.where(visible[None], bias, NEG_INF)


def _mix_body(hg_ref, oa_ref, gm_ref, ga_ref, x_ref, wbm_ref, wba_ref, wo_ref, g_ref, b_ref, o_ref):
    ym = _dot(hg_ref[...], wbm_ref[...])
    ya = _dot(oa_ref[...], wba_ref[...])
    gated = gm_ref[...].astype(F32) * ym + ga_ref[...].astype(F32) * ya
    mix = _dot(gated.astype(BF16), wo_ref[...])
    o_ref[...] = _layer_norm(ALPHA * x_ref[...] + mix, g_ref[...], b_ref[...])


def _mix(hg, oa, sg3, x, wbm, wba, wo, ln_g, ln_b, tb=1024):
    t = x.shape[0]
    row = lambda i: (i, 0)
    const = lambda i: (0, 0)
    return pl.pallas_call(
        _mix_body,
        grid=(t // tb,),
        in_specs=[pl.BlockSpec((tb, M_WIDTH), row),
                  pl.BlockSpec((tb, A_WIDTH), row),
                  pl.BlockSpec((tb, D_MODEL), lambda i: (i, 1)),
                  pl.BlockSpec((tb, D_MODEL), lambda i: (i, 2)),
                  pl.BlockSpec((tb, D_MODEL), row),
                  pl.BlockSpec((M_WIDTH, D_MODEL), const),
                  pl.BlockSpec((A_WIDTH, D_MODEL), const),
                  pl.BlockSpec((D_MODEL, D_MODEL), const),
                  pl.BlockSpec((1, D_MODEL), const),
                  pl.BlockSpec((1, D_MODEL), const)],
        out_specs=pl.BlockSpec((tb, D_MODEL), row),
        out_shape=jax.ShapeDtypeStruct((t, D_MODEL), F32),
        compiler_params=_cparams(("parallel",)),
    )(hg, oa, sg3, sg3, x, wbm, wba, wo, ln_g, ln_b)


SUBLANES = 8
LOG2E = math.log2(math.e)


def _batcher_network(lo, hi):
    def merge(lo, hi, r):
        step = r * 2
        if step < hi - lo:
            yield from merge(lo, hi, step)
            yield from merge(lo + r, hi, step)
            yield from [(i, i + r) for i in range(lo + r, hi - r, step)]
        else:
            yield (lo, lo + r)
    if hi - lo >= 1:
        mid = lo + (hi - lo) // 2
        yield from _batcher_network(lo, mid)
        yield from _batcher_network(mid + 1, hi)
        yield from merge(lo, hi, 1)


_SORT16 = tuple(_batcher_network(0, TOPK - 1))
_BITONIC16 = tuple((i, i + s) for s in (8, 4, 2, 1) for i in range(TOPK) if (i // s) % 2 == 0)


def _compare_exchange(rows, net):
    rows = list(rows)
    for i, j in net:
        a, b = rows[i], rows[j]
        if b is None:
            continue
        if a is None:
            rows[i], rows[j] = b, None
        else:
            rows[i], rows[j] = jnp.maximum(a, b), jnp.minimum(a, b)
    return rows


def _top16_values(slabs):
    rows = list(slabs) + [None] * (TOPK - len(slabs))
    rows = _compare_exchange(rows, _SORT16)
    for shift in (4, 2, 1):
        merged = []
        for r in range(TOPK):
            a, b = rows[r], rows[TOPK - 1 - r]
            b = None if b is None else pltpu.roll(b, shift, axis=0)
            merged.append(b if a is None else a if b is None else jnp.maximum(a, b))
        rows = _compare_exchange(merged, _BITONIC16)
    return rows


def _by_sublane(rows, sub):
    out = rows[SUBLANES - 1]
    for g in range(SUBLANES - 2, -1, -1):
        out = jnp.where(sub == g, rows[g], out)
    return out


def _peer_sel_body(x_ref, wq_ref, sk_ref, s_ref, aux_ref):
    tb = x_ref.shape[0]
    qt = _dot_nt(wq_ref[...], x_ref[...].astype(BF16))
    sub = lax.broadcasted_iota(jnp.int32, (SUBLANES, tb), 0)
    for h in range(R_HEADS):
        top = []
        for p in range(2):
            r = h * 2 + p
            qhp = qt[r * HALF_KEY:(r + 1) * HALF_KEY, :].astype(BF16)
            st = _dot(sk_ref[r], qhp) * LOG2E
            s_ref[r] = st
            top.append(_top16_values([st[g * SUBLANES:(g + 1) * SUBLANES, :] for g in range(N_KEYS // SUBLANES)]))
        c1, c2 = top
        v1_lo, v1_hi = _by_sublane(c1[:SUBLANES], sub), _by_sublane(c1[SUBLANES:], sub)
        v2_hi = _by_sublane(c2[SUBLANES:], sub)
        firsts = [v1_lo] * SUBLANES + [c1[0], v1_hi]
        seconds = c2[:SUBLANES] + [v2_hi, c2[0]]
        cand = [f + s for f, s in zip(firsts, seconds)]
        cv = _top16_values(cand)
        tau, best = cv[TOPK - 1], cv[0]
        picked = [c >= tau for c in cand]
        z = sum(jnp.where(pk, jnp.exp2(c - best), 0.0) for pk, c in zip(picked, cand))
        shift = best[0:1, :] + jnp.log2(jnp.sum(z, axis=0, keepdims=True))
        shifted = [jnp.where(pk, (f - shift) + s, jnp.inf) for pk, f, s in zip(picked, firsts, seconds)]
        tau_shifted = functools.reduce(jnp.minimum, shifted)
        s_ref[2 * h] = s_ref[2 * h] - shift
        aux_ref[h:h + 1, :] = jnp.min(tau_shifted, axis=0, keepdims=True)


def _peer_sel(x, wq_t, subkeys, tb=512):
    t = x.shape[0]
    return pl.pallas_call(
        _peer_sel_body,
        grid=(t // tb,),
        in_specs=[pl.BlockSpec((tb, D_MODEL), lambda i: (i, 0)),
                  pl.BlockSpec((2 * R_HEADS * HALF_KEY, D_MODEL), lambda i: (0, 0)),
                  pl.BlockSpec((2 * R_HEADS, N_KEYS, HALF_KEY), lambda i: (0, 0, 0))],
        out_specs=[pl.BlockSpec((2 * R_HEADS, N_KEYS, tb), lambda i: (0, 0, i)),
                   pl.BlockSpec((R_HEADS, tb), lambda i: (0, i))],
        out_shape=[jax.ShapeDtypeStruct((2 * R_HEADS, N_KEYS, t), F32),
                   jax.ShapeDtypeStruct((R_HEADS, t), F32)],
        compiler_params=_cparams(("parallel",)),
    )(x, wq_t, subkeys)


def _gelu_tanh(x):
    return 0.5 * x * (1.0 + jnp.tanh(math.sqrt(2.0 / math.pi) * (x + 0.044715 * (x * x * x))))


PEER_TILE = SUBLANES * N_KEYS
PEER_TILES = N_EXPERTS // PEER_TILE


def _peer_ffn_body(x_ref, u_ref, vt_ref, s_ref, aux_ref, pe_ref, plp_ref, plg_ref, g_ref, b_ref,
                   o_ref, xb_scr, acc_scr, p_scr):
    e = pl.program_id(1)
    tb = x_ref.shape[0]

    @pl.when(e == 0)
    def _():
        xb_scr[...] = x_ref[...].astype(BF16)
        acc_scr[...] = jnp.zeros_like(acc_scr)

    act = _gelu_tanh(_dot_nt(u_ref[...], xb_scr[...]).astype(BF16))
    i0 = pl.multiple_of(e * SUBLANES, SUBLANES)
    for ii in range(SUBLANES):
        rs = slice(ii * N_KEYS, (ii + 1) * N_KEYS)
        for lg in range(tb // LANES):
            ls = slice(lg * LANES, (lg + 1) * LANES)
            w = jnp.zeros((N_KEYS, LANES), F32)
            for h in range(R_HEADS):
                s1 = s_ref[2 * h, pl.ds(i0, SUBLANES), ls][ii:ii + 1, :]
                s = s1 + s_ref[2 * h + 1, :, ls]
                w = w + jnp.where(s >= aux_ref[h:h + 1, ls], jnp.exp2(s), 0.0)
            p_scr[rs, ls] = act[rs, ls] * w.astype(BF16)
    acc_scr[...] += _dot(vt_ref[0], p_scr[...])

    @pl.when(e == pl.num_programs(1) - 1)
    def _():
        x = x_ref[...]
        y = _layer_norm(ALPHA * x + acc_scr[...].T, g_ref[...], b_ref[...])
        ple = _dot(pe_ref[...].astype(BF16), plp_ref[...])
        gate = _sigmoid(_dot(y.astype(BF16), plg_ref[...]))
        o_ref[...] = y + ple * gate


def _peer_ffn(x, u, vt_tiles, scores, aux, pe, plp, plg, ln_g, ln_b, tb=512):
    t = x.shape[0]
    row = lambda i, e: (i, 0)
    const = lambda i, e: (0, 0)
    return pl.pallas_call(
        _peer_ffn_body,
        grid=(t // tb, PEER_TILES),
        in_specs=[pl.BlockSpec((tb, D_MODEL), row),
                  pl.BlockSpec((PEER_TILE, D_MODEL), lambda i, e: (e, 0)),
                  pl.BlockSpec((1, D_MODEL, PEER_TILE), lambda i, e: (e, 0, 0)),
                  pl.BlockSpec((2 * R_HEADS, N_KEYS, tb), lambda i, e: (0, 0, i)),
                  pl.BlockSpec((R_HEADS, tb), lambda i, e: (0, i)),
                  pl.BlockSpec((tb, PLE_DIM), row),
                  pl.BlockSpec((PLE_DIM, D_MODEL), const),
                  pl.BlockSpec((D_MODEL, D_MODEL), const),
                  pl.BlockSpec((1, D_MODEL), const),
                  pl.BlockSpec((1, D_MODEL), const)],
        out_specs=pl.BlockSpec((tb, D_MODEL), row),
        out_shape=jax.ShapeDtypeStruct((t, D_MODEL), F32),
        scratch_shapes=[pltpu.VMEM((tb, D_MODEL), BF16),
                        pltpu.VMEM((D_MODEL, tb), F32),
                        pltpu.VMEM((PEER_TILE, tb), BF16)],
        compiler_params=_cparams(("parallel", "arbitrary")),
    )(x, u, vt_tiles, scores, aux, pe, plp, plg, ln_g, ln_b)


def _prep_layer(w_in, if_bias, norm_w, rel_bias, w_bm, w_ba, w_o, ln1g, ln1b, ln2g, ln2b,
                pq, psk, pu, pv, plp, plg):
    c = 0
    cols = {}
    for name, width in (("mq", M_WIDTH), ("mk", M_WIDTH), ("mv", M_WIDTH), ("mi", M_HEADS), ("mf", M_HEADS),
                        ("mo", M_WIDTH), ("aq", A_WIDTH), ("ak", A_WIDTH), ("av", A_WIDTH),
                        ("gm", D_MODEL), ("ga", D_MODEL)):
        cols[name] = w_in[:, c:c + width]
        c += width
    cat = lambda names: jnp.concatenate([cols[n] for n in names], axis=1)
    ones = lambda n: jnp.ones((1, n), F32)
    w_if = jnp.pad(cat(("mi", "mf")), ((0, 0), (0, LANES - 2 * M_HEADS)))
    b_if = jnp.pad(if_bias.reshape(1, 2 * M_HEADS).astype(F32), ((0, 0), (0, LANES - 2 * M_HEADS)))
    return dict(
        w_mqkv=cat(("mq", "mk", "mv")).astype(BF16),
        s_mqkv=jnp.concatenate([ones(M_WIDTH), ones(M_WIDTH) * (M_HDIM ** -0.5), ones(M_WIDTH)], axis=1),
        w_sig=cat(("mo", "gm", "ga")).astype(BF16), s_sig=ones(3 * D_MODEL),
        w_aqkv=cat(("aq", "ak", "av")).astype(BF16), s_aqkv=ones(3 * A_WIDTH),
        w_if=w_if, b_if=b_if,
        norm_w=norm_w.reshape(1, M_WIDTH).astype(F32),
        bias=_attn_bias(rel_bias),
        w_bm=w_bm.astype(BF16), w_ba=w_ba.astype(BF16), w_o=w_o.astype(BF16),
        ln1g=ln1g.reshape(1, D_MODEL), ln1b=ln1b.reshape(1, D_MODEL),
        ln2g=ln2g.reshape(1, D_MODEL), ln2b=ln2b.reshape(1, D_MODEL),
        wq_t=pq.T.astype(BF16),
        subkeys=psk.reshape(2 * R_HEADS, N_KEYS, HALF_KEY).astype(BF16),
        u=pu.astype(BF16),
        vt=pv.reshape(PEER_TILES, PEER_TILE, D_MODEL).transpose(0, 2, 1).astype(BF16),
        plp=plp.astype(BF16), plg=plg.astype(BF16),
    )


def _layer(x3, pe3, lw, state, cache):
    b, s, _ = x3.shape
    t = b * s
    x = x3.reshape(t, D_MODEL)
    mqkv, = _proj(x, lw["w_mqkv"], lw["s_mqkv"], None, (BF16,), tb=1024, nb=1024)
    sig3, = _proj(x, lw["w_sig"], lw["s_sig"], "sigmoid", (BF16,), tb=1024, nb=1024)
    aqkv, akv32 = _proj(x, lw["w_aqkv"], lw["s_aqkv"], None, (BF16, F32), tb=1024, nb=512)
    gates = _gates(x, lw["w_if"], lw["b_if"])

    if state is None:
        c0 = jnp.zeros((b, M_HEADS, M_HDIM, M_HDIM), F32)
        n0 = jnp.zeros((b, 1, M_WIDTH), F32)
        m0 = jnp.zeros((b, 1, LANES), F32)
    else:
        c0 = state[0].astype(F32)
        n0 = state[1].astype(F32).reshape(b, 1, M_WIDTH)
        m0 = jnp.pad(state[2].astype(F32), ((0, 0), (0, LANES - M_HEADS))).reshape(b, 1, LANES)
    hg, c_new, n_new, m_new = _mlstm(mqkv.reshape(b, s, 3 * M_WIDTH), gates.reshape(b, s, LANES),
                                     sig3.reshape(b, s, 3 * D_MODEL), lw["norm_w"], c0, n0, m0)
    n_new = n_new.reshape(b, M_HEADS, M_HDIM)
    m_new = m_new.reshape(b, LANES)[:, :M_HEADS]

    a3 = aqkv.reshape(b, s, 3 * A_WIDTH)
    q, k, v = a3[..., :A_WIDTH], a3[..., A_WIDTH:2 * A_WIDTH], a3[..., 2 * A_WIDTH:]
    kv32 = akv32.reshape(b, s, 3 * A_WIDTH)
    if cache is None:
        hist_k = jnp.zeros((b, ATT_REACH, A_WIDTH), BF16)
        hist_v = hist_k
        first_valid_block = ATT_REACH // ATT_Q
        keep = min(ATT_REACH, s)
        k_out = kv32[:, s - keep:, A_WIDTH:2 * A_WIDTH].reshape(b, keep, A_HEADS, A_HDIM)
        v_out = kv32[:, s - keep:, 2 * A_WIDTH:].reshape(b, keep, A_HEADS, A_HDIM)
    else:
        hist_k = cache[0].reshape(b, ATT_REACH, A_WIDTH).astype(BF16)
        hist_v = cache[1].reshape(b, ATT_REACH, A_WIDTH).astype(BF16)
        first_valid_block = 0
        k_out = kv32[..., A_WIDTH:2 * A_WIDTH].reshape(b, s, A_HEADS, A_HDIM)
        v_out = kv32[..., 2 * A_WIDTH:].reshape(b, s, A_HEADS, A_HDIM)
    sq = -(-s // ATT_Q) * ATT_Q
    tail = ((0, 0), (0, sq - s), (0, 0))
    oa = _attn(jnp.pad(q, tail), jnp.pad(jnp.concatenate([hist_k, k], axis=1), tail),
               jnp.pad(jnp.concatenate([hist_v, v], axis=1), tail), lw["bias"], first_valid_block)
    oa = oa[:, :s].reshape(t, A_WIDTH)

    x1 = _mix(hg.reshape(t, M_WIDTH), oa, sig3, x, lw["w_bm"], lw["w_ba"], lw["w_o"], lw["ln1g"], lw["ln1b"])
    scores, aux = _peer_sel(x1, lw["wq_t"], lw["subkeys"])
    x3_out = _peer_ffn(x1, lw["u"], lw["vt"], scores, aux, pe3.reshape(t, PLE_DIM), lw["plp"], lw["plg"],
                       lw["ln2g"], lw["ln2b"])
    return x3_out.reshape(b, s, D_MODEL), (c_new, n_new, m_new), (k_out, v_out)


def kernel(x_prompt, x_sample, cache_attn_k, cache_attn_v, state_mlstm_C, state_mlstm_n, state_mlstm_m,
           p_prompt, p_sample, w_in, mlstm_if_bias, mlstm_norm_w, attn_rel_bias, w_branch_m, w_branch_a,
           w_out, ln1_g, ln1_b, ln2_g, ln2_b, peer_wq, peer_subkeys, peer_u, peer_v, ple_proj, ple_gate):
    layer_w = (w_in, mlstm_if_bias, mlstm_norm_w, attn_rel_bias, w_branch_m, w_branch_a, w_out,
               ln1_g, ln1_b, ln2_g, ln2_b, peer_wq, peer_subkeys, peer_u, peer_v, ple_proj, ple_gate)
    yp, ys = x_prompt, x_sample
    outs_p = [[] for _ in range(5)]
    outs_s = [[] for _ in range(5)]
    for i in range(DEPTH):
        lw = _prep_layer(*(w[i] for w in layer_w))
        yp, (cp, np_, mp), (kp, vp) = _layer(yp, p_prompt[i], lw, None, None)
        ys, (cs, ns, ms), (ks, vs) = _layer(
            ys, p_sample[i], lw, (state_mlstm_C[i], state_mlstm_n[i], state_mlstm_m[i]),
            (cache_attn_k[i], cache_attn_v[i]))
        for lst, val in zip(outs_p, (kp, vp, cp, np_, mp)):
            lst.append(val)
        for lst, val in zip(outs_s, (ks, vs, cs, ns, ms)):
            lst.append(val)
    st = lambda l: jnp.stack(l, axis=0)
    return (yp, ys) + tuple(st(l) for l in outs_p) + tuple(st(l) for l in outs_s)
```
